```python
import jax, jax.numpy as jnp
from jax import lax
import numpy as np

D_MODEL = 1024
BATCH = 8
SEQ = 4096
DEPTH = 1

N_ATT_HEADS = 8
ATT_HEAD_DIM = 64
ATT_WIDTH = N_ATT_HEADS * ATT_HEAD_DIM
KV_LATENT = 128
IDX_HEADS = 4
IDX_DIM = 64
TOPK_MAX = 256
Q_BLOCK = 128
N_MLP_GROUPS = 8
MLP_GROUP_DIM = 64
MLP_WIDTH = N_MLP_GROUPS * MLP_GROUP_DIM
CHUNK = 128
MIX_WIDTH = ATT_WIDTH + MLP_WIDTH
D_FF = 2816
DEEPNORM_ALPHA = float((2 * DEPTH) ** 0.25)
DEEPNORM_BETA = float((8 * DEPTH) ** -0.25)
LN_EPS = 1e-5
IN_COLS = ATT_WIDTH + KV_LATENT + IDX_HEADS * IDX_DIM + IDX_DIM + IDX_HEADS + 2 * MLP_WIDTH
SPLITS = tuple(np.cumsum([ATT_WIDTH, KV_LATENT, IDX_HEADS * IDX_DIM, IDX_DIM, IDX_HEADS]).tolist())

kernel_name = "hybrid_dsa_gmlp_macaron_deepnorm"


def _layernorm(x, g, b):
    xf = x.astype(jnp.float32)
    mu = jnp.mean(xf, axis=-1, keepdims=True)
    var = jnp.mean(jnp.square(xf - mu), axis=-1, keepdims=True)
    y = (xf - mu) * lax.rsqrt(var + LN_EPS)
    return (y * g.astype(jnp.float32) + b.astype(jnp.float32)).astype(x.dtype)


def _rmsnorm(x, g):
    xf = x.astype(jnp.float32)
    y = xf * lax.rsqrt(jnp.mean(jnp.square(xf), axis=-1, keepdims=True) + LN_EPS)
    return (y * g.astype(jnp.float32)).astype(x.dtype)


def _swiglu(x, w1, w3, w2):
    return (jax.nn.silu(x @ w1) * (x @ w3)) @ w2


def _sparse_attention(q, c, qi, ki, wi, w_uk, w_uv, topk):
    B, L = q.shape[0], q.shape[1]
    q_abs = jnp.einsum('blhd,hcd->blhc', q, w_uk) * (ATT_HEAD_DIM ** -0.5)
    wi_s = wi.astype(jnp.float32) * (IDX_HEADS ** -0.5)
    ki_f = ki.astype(jnp.float32)
    key_pos = jnp.arange(L)

    def block(i):
        start = i * Q_BLOCK
        qa = lax.dynamic_slice_in_dim(q_abs, start, Q_BLOCK, axis=1)
        qib = lax.dynamic_slice_in_dim(qi, start, Q_BLOCK, axis=1)
        wib = lax.dynamic_slice_in_dim(wi_s, start, Q_BLOCK, axis=1)
        t = start + jnp.arange(Q_BLOCK)
        causal = key_pos[None, :] <= t[:, None]
        logit = jnp.einsum('bqhd,bsd->bqhs', qib.astype(jnp.float32), ki_f) * (IDX_DIM ** -0.5)
        score = jnp.einsum('bqh,bqhs->bqs', wib, jax.nn.relu(logit))
        score = jnp.where(causal[None], score, -jnp.inf)
        _, sel = lax.top_k(score, topk)
        c_sel = jax.vmap(lambda cb, ib: cb[ib])(c, sel)
        valid = sel <= t[None, :, None]
        sc = jnp.einsum('bqhc,bqkc->bqhk', qa.astype(jnp.float32), c_sel.astype(jnp.float32))
        sc = jnp.where(valid[:, :, None, :], sc, -jnp.inf)
        p = jax.nn.softmax(sc, axis=-1).astype(c.dtype)
        o_lat = jnp.einsum('bqhk,bqkc->bqhc', p, c_sel)
        return jnp.einsum('bqhc,hcd->bqhd', o_lat, w_uv)

    out = lax.map(block, jnp.arange(L // Q_BLOCK))
    return out.transpose(1, 0, 2, 3, 4).reshape(B, L, ATT_WIDTH)


def _chunked_gmlp(z, gln_g, gln_b, ws, bs):
    B, L = z.shape[0], z.shape[1]
    z = jax.nn.gelu(z)
    u, v = jnp.split(z, 2, axis=-1)
    v = _layernorm(v, gln_g, gln_b)
    v = v.reshape(B, L // CHUNK, CHUNK, N_MLP_GROUPS, MLP_GROUP_DIM)
    tri = jnp.tril(jnp.ones((CHUNK, CHUNK), dtype=bool))
    ws_c = jnp.where(tri[None], ws, jnp.zeros_like(ws))
    sv = jnp.einsum('gts,bnsgd->bntgd', ws_c, v) + bs.T[None, None, :, :, None]
    return u * sv.reshape(B, L, MLP_WIDTH)


def _mixer(h, w_in, kv_norm_g, w_uk, w_uv, gln_g, gln_b, ws, bs, w_out, topk):
    B, L, _ = h.shape
    proj = h @ w_in
    q, c, qi, ki, wi, z = jnp.split(proj, SPLITS, axis=-1)
    q = q.reshape(B, L, N_ATT_HEADS, ATT_HEAD_DIM)
    c = _rmsnorm(c, kv_norm_g)
    qi = qi.reshape(B, L, IDX_HEADS, IDX_DIM)
    att = _sparse_attention(q, c, qi, ki, wi, w_uk, w_uv, topk)
    gm = _chunked_gmlp(z, gln_g, gln_b, ws, bs)
    return jnp.concatenate([att, gm], axis=-1) @ w_out


def setup_inputs(seed: int = 0) -> dict:
    key = jax.random.key(seed)
    ks = jax.random.split(key, 24)
    f32 = jnp.float32

    def nrm(k, shape, scale):
        return jax.random.normal(k, shape, f32) * scale

    def gain(k, shape):
        return 1.0 + 0.01 * jax.random.normal(k, shape, f32)

    L_ = DEPTH
    return {
        "x": jax.random.normal(ks[0], (BATCH, SEQ, D_MODEL), f32),
        "ffn1_w1": nrm(ks[1], (L_, D_MODEL, D_FF), D_MODEL ** -0.5),
        "ffn1_w3": nrm(ks[2], (L_, D_MODEL, D_FF), D_MODEL ** -0.5),
        "ffn1_w2": nrm(ks[3], (L_, D_FF, D_MODEL), DEEPNORM_BETA * D_FF ** -0.5),
        "ln1_g": gain(ks[4], (L_, D_MODEL)),
        "ln1_b": nrm(ks[5], (L_, D_MODEL), 0.01),
        "w_in": nrm(ks[6], (L_, D_MODEL, IN_COLS), D_MODEL ** -0.5),
        "kv_norm_g": gain(ks[7], (L_, KV_LATENT)),
        "w_uk": nrm(ks[8], (L_, N_ATT_HEADS, KV_LATENT, ATT_HEAD_DIM), KV_LATENT ** -0.5),
        "w_uv": nrm(ks[9], (L_, N_ATT_HEADS, KV_LATENT, ATT_HEAD_DIM), DEEPNORM_BETA * KV_LATENT ** -0.5),
        "gmlp_ln_g": gain(ks[10], (L_, MLP_WIDTH)),
        "gmlp_ln_b": nrm(ks[11], (L_, MLP_WIDTH), 0.01),
        "gmlp_ws": nrm(ks[12], (L_, N_MLP_GROUPS, CHUNK, CHUNK), 0.5 * CHUNK ** -0.5),
        "gmlp_bs": gain(ks[13], (L_, N_MLP_GROUPS, CHUNK)),
        "w_out": nrm(ks[14], (L_, MIX_WIDTH, D_MODEL), DEEPNORM_BETA * MIX_WIDTH ** -0.5),
        "ln2_g": gain(ks[15], (L_, D_MODEL)),
        "ln2_b": nrm(ks[16], (L_, D_MODEL), 0.01),
        "ffn2_w1": nrm(ks[17], (L_, D_MODEL, D_FF), D_MODEL ** -0.5),
        "ffn2_w3": nrm(ks[18], (L_, D_MODEL, D_FF), D_MODEL ** -0.5),
        "ffn2_w2": nrm(ks[19], (L_, D_FF, D_MODEL), DEEPNORM_BETA * D_FF ** -0.5),
        "ln3_g": gain(ks[20], (L_, D_MODEL)),
        "ln3_b": nrm(ks[21], (L_, D_MODEL), 0.01),
    }


def reference(x, ffn1_w1, ffn1_w3, ffn1_w2, ln1_g, ln1_b, w_in, kv_norm_g, w_uk, w_uv,
              gmlp_ln_g, gmlp_ln_b, gmlp_ws, gmlp_bs, w_out, ln2_g, ln2_b,
              ffn2_w1, ffn2_w3, ffn2_w2, ln3_g, ln3_b):
    L = x.shape[1]
    topk = min(TOPK_MAX, L // 4)
    for l in range(DEPTH):
        x = _layernorm(DEEPNORM_ALPHA * x + 0.5 * _swiglu(x, ffn1_w1[l], ffn1_w3[l], ffn1_w2[l]),
                       ln1_g[l], ln1_b[l])
        m = _mixer(x, w_in[l], kv_norm_g[l], w_uk[l], w_uv[l], gmlp_ln_g[l], gmlp_ln_b[l],
                   gmlp_ws[l], gmlp_bs[l], w_out[l], topk)
        x = _layernorm(DEEPNORM_ALPHA * x + m, ln2_g[l], ln2_b[l])
        x = _layernorm(DEEPNORM_ALPHA * x + 0.5 * _swiglu(x, ffn2_w1[l], ffn2_w3[l], ffn2_w2[l]),
                       ln3_g[l], ln3_b[l])
    return x
```

```python
import functools

import jax
import jax.numpy as jnp
import numpy as np
from jax import lax
from jax.experimental import pallas as pl
from jax.experimental.pallas import tpu as pltpu

D_MODEL = 1024
N_ATT_HEADS = 8
ATT_HEAD_DIM = 64
ATT_WIDTH = N_ATT_HEADS * ATT_HEAD_DIM
KV_LATENT = 128
IDX_HEADS = 4
IDX_DIM = 64
TOPK_MAX = 256
N_MLP_GROUPS = 8
MLP_GROUP_DIM = 64
MLP_WIDTH = N_MLP_GROUPS * MLP_GROUP_DIM
CHUNK = 128
D_FF = 2816
LN_EPS = 1e-5

LANES = 128
Q_TILE = 128
KEY_CHUNK = 512
FFN_TILE = 512
FF_CHUNK = 1408
PROJ_TILE = 256
BISECT_STEPS = 24
VMEM_LIMIT = 56 * 1024 * 1024
NEG_BIG = -1e30
F32_LOWEST = float(np.finfo(np.float32).min)

_PC_Q = 0
_PC_C = ATT_WIDTH
_PC_QI = _PC_C + KV_LATENT
_PC_KW = _PC_QI + IDX_HEADS * IDX_DIM
_PC_Z = _PC_KW + LANES
_PC_END = _PC_Z + 2 * MLP_WIDTH

bf16 = jnp.bfloat16
f32 = jnp.float32


def _const_spec(shape):
    nd = len(shape)
    return pl.BlockSpec(shape, lambda *_: (0,) * nd, pipeline_mode=pl.Buffered(1))


def _layernorm(y, g, b):
    mu = jnp.mean(y, axis=-1, keepdims=True)
    d = y - mu
    var = jnp.mean(d * d, axis=-1, keepdims=True)
    return d * lax.rsqrt(var + LN_EPS) * g + b


def _swiglu(xb, w1_ref, w3_ref, w2_ref):
    acc = None
    for c0 in range(0, D_FF, FF_CHUNK):
        a = jnp.dot(xb, w1_ref[:, c0:c0 + FF_CHUNK], preferred_element_type=f32)
        b = jnp.dot(xb, w3_ref[:, c0:c0 + FF_CHUNK], preferred_element_type=f32)
        h = (jax.nn.silu(a) * b).astype(bf16)
        part = jnp.dot(h, w2_ref[c0:c0 + FF_CHUNK, :], preferred_element_type=f32)
        acc = part if acc is None else acc + part
    return acc


def _ffn_ln_kernel(alpha, x_ref, w1_ref, w3_ref, w2_ref, g_ref, b_ref, o_ref):
    x = x_ref[...]
    y = alpha * x + 0.5 * _swiglu(x.astype(bf16), w1_ref, w3_ref, w2_ref)
    o_ref[...] = _layernorm(y, g_ref[...], b_ref[...])


def _ffn_ln(x, w1, w3, w2, g, b, alpha):
    T = x.shape[0]
    row = pl.BlockSpec((FFN_TILE, D_MODEL), lambda i: (i, 0))
    return pl.pallas_call(
        functools.partial(_ffn_ln_kernel, alpha),
        grid=(T // FFN_TILE,),
        in_specs=[row, _const_spec(w1.shape), _const_spec(w3.shape), _const_spec(w2.shape),
                  _const_spec(g.shape), _const_spec(b.shape)],
        out_specs=row,
        out_shape=jax.ShapeDtypeStruct((T, D_MODEL), f32),
        compiler_params=pltpu.CompilerParams(dimension_semantics=("arbitrary",),
                                             vmem_limit_bytes=VMEM_LIMIT),
        name="ffn_ln",
    )(x, w1, w3, w2, g, b)


def _proj_kernel(h_ref, wp_ref, kvg_ref, wuk_ref, glng_ref, glnb_ref, ws_ref, bs_ref,
                 qabs_ref, c_ref, ct_ref, qi_ref, kit_ref, kw_ref, gm_ref):
    hb = h_ref[...].astype(bf16)
    proj = jnp.dot(hb, wp_ref[...], preferred_element_type=f32)

    for j in range(N_ATT_HEADS // 2):
        qj = proj[:, _PC_Q + j * LANES:_PC_Q + (j + 1) * LANES].astype(bf16)
        qa = jnp.dot(qj, wuk_ref[j], preferred_element_type=f32) * (ATT_HEAD_DIM ** -0.5)
        qabs_ref[:, 2 * j * LANES:2 * (j + 1) * LANES] = qa.astype(bf16)

    c = proj[:, _PC_C:_PC_C + KV_LATENT]
    c = c * lax.rsqrt(jnp.mean(c * c, axis=-1, keepdims=True) + LN_EPS) * kvg_ref[...]
    c_ref[...] = c.astype(bf16)
    ct_ref[...] = c.T.astype(bf16)

    qi_ref[...] = proj[:, _PC_QI:_PC_QI + IDX_HEADS * IDX_DIM].astype(bf16)
    kw = proj[:, _PC_KW:_PC_KW + LANES]
    kw_ref[...] = kw
    kit_ref[...] = kw.T[:IDX_DIM, :].astype(bf16)

    z = jax.nn.gelu(proj[:, _PC_Z:_PC_END])
    u = z[:, :MLP_WIDTH]
    v = _layernorm(z[:, MLP_WIDTH:], glng_ref[...], glnb_ref[...]).astype(bf16)
    r = lax.broadcasted_iota(jnp.int32, (CHUNK, CHUNK), 0)
    s = lax.broadcasted_iota(jnp.int32, (CHUNK, CHUNK), 1)
    tri = s <= r
    left = lax.broadcasted_iota(jnp.int32, (CHUNK, LANES), 1) < MLP_GROUP_DIM
    for t0 in range(0, PROJ_TILE, CHUNK):
        for j in range(N_MLP_GROUPS // 2):
            vj = v[t0:t0 + CHUNK, j * LANES:(j + 1) * LANES]
            wa = jnp.where(tri, ws_ref[2 * j], 0.0).astype(bf16)
            wb = jnp.where(tri, ws_ref[2 * j + 1], 0.0).astype(bf16)
            sa = jnp.dot(wa, vj, preferred_element_type=f32)
            sb = jnp.dot(wb, vj, preferred_element_type=f32)
            sv = jnp.where(left, sa, sb) + bs_ref[:, j * LANES:(j + 1) * LANES]
            gm_ref[t0:t0 + CHUNK, j * LANES:(j + 1) * LANES] = (
                u[t0:t0 + CHUNK, j * LANES:(j + 1) * LANES] * sv).astype(bf16)


def _proj(h, wp, kvg, wuk2, glng, glnb, ws, bs_full):
    T = h.shape[0]
    tm = PROJ_TILE

    def rows(n, dt):
        return pl.BlockSpec((tm, n), lambda i: (i, 0)), jax.ShapeDtypeStruct((T, n), dt)

    def cols(n, dt):
        return pl.BlockSpec((n, tm), lambda i: (0, i)), jax.ShapeDtypeStruct((n, T), dt)

    outs = [rows(N_ATT_HEADS * KV_LATENT, bf16),
            rows(KV_LATENT, bf16),
            cols(KV_LATENT, bf16),
            rows(IDX_HEADS * IDX_DIM, bf16),
            cols(IDX_DIM, bf16),
            rows(LANES, f32),
            rows(MLP_WIDTH, bf16)]
    return pl.pallas_call(
        _proj_kernel,
        grid=(T // tm,),
        in_specs=[pl.BlockSpec((tm, D_MODEL), lambda i: (i, 0)),
                  _const_spec(wp.shape), _const_spec(kvg.shape), _const_spec(wuk2.shape),
                  _const_spec(glng.shape), _const_spec(glnb.shape), _const_spec(ws.shape),
                  _const_spec(bs_full.shape)],
        out_specs=[o[0] for o in outs],
        out_shape=[o[1] for o in outs],
        compiler_params=pltpu.CompilerParams(dimension_semantics=("arbitrary",),
                                             vmem_limit_bytes=VMEM_LIMIT),
        name="proj",
    )(h, wp, kvg, wuk2, glng, glnb, ws, bs_full)


def _attn_kernel(topk, qabs_ref, qi_ref, kw_ref, ct_ref, c_ref, kit_ref, wuv_ref, o_ref,
                 s_ref, tau_ref):
    i = pl.program_id(1)
    nchunk = i // (KEY_CHUNK // Q_TILE) + 1
    sub = KEY_CHUNK // LANES
    row = i * Q_TILE + lax.broadcasted_iota(jnp.int32, (Q_TILE, 1), 0)
    kf = float(topk)

    qi = qi_ref[...]
    qih = [qi[:, h * IDX_DIM:(h + 1) * IDX_DIM] for h in range(IDX_HEADS)]
    kw = kw_ref[...]
    wih = [kw[:, IDX_DIM + h:IDX_DIM + h + 1] * (IDX_HEADS ** -0.5) for h in range(IDX_HEADS)]

    def score_chunk(j, carry):
        k0 = pl.multiple_of(j * KEY_CHUNK, KEY_CHUNK)
        kt = kit_ref[:, pl.ds(k0, KEY_CHUNK)]
        s = None
        for h in range(IDX_HEADS):
            lg = jnp.dot(qih[h], kt, preferred_element_type=f32) * (IDX_DIM ** -0.5)
            term = wih[h] * jnp.maximum(lg, 0.0)
            s = term if s is None else s + term
        col = k0 + lax.broadcasted_iota(jnp.int32, (1, KEY_CHUNK), 1)
        s_ref[:, pl.ds(k0, KEY_CHUNK)] = jnp.where(col <= row, s, -jnp.inf)
        return carry

    lax.fori_loop(0, nchunk, score_chunk, 0)

    def reduce_rows(init, fn):
        def body(j, acc):
            k0 = pl.multiple_of(j * KEY_CHUNK, KEY_CHUNK)
            for t in range(sub):
                acc = fn(acc, s_ref[:, pl.ds(k0 + t * LANES, LANES)])
            return acc
        return lax.fori_loop(0, nchunk, body, jnp.full((Q_TILE, LANES), init, f32))

    def count(pred):
        acc = reduce_rows(0.0, lambda a, s: a + jnp.where(pred(s), 1.0, 0.0))
        return jnp.sum(acc, axis=1, keepdims=True)

    @pl.when(i * Q_TILE + Q_TILE <= topk)
    def _():
        tau_ref[...] = jnp.full((Q_TILE, 1), F32_LOWEST, f32)

    @pl.when(i * Q_TILE + Q_TILE > topk)
    def _():
        mx = jnp.max(reduce_rows(-jnp.inf, jnp.maximum), axis=1, keepdims=True)
        mn = jnp.min(reduce_rows(jnp.inf, lambda a, s: jnp.minimum(
            a, jnp.where(s == -jnp.inf, jnp.inf, s))), axis=1, keepdims=True)
        top = count(lambda s: s >= mx) >= kf
        lo = jnp.where(top, mx, mn)
        hi = jnp.where(top, jnp.inf, mx)

        def bisect(_, lh):
            lo, hi = lh
            mid = 0.5 * lo + 0.5 * hi
            ge = count(lambda s: s >= mid) >= kf
            return jnp.where(ge, mid, lo), jnp.where(ge, hi, mid)

        def bracket(lo, hi):
            a = jnp.min(reduce_rows(jnp.inf, lambda acc, s: jnp.minimum(
                acc, jnp.where(s >= lo, s, jnp.inf))), axis=1, keepdims=True)
            b = jnp.max(reduce_rows(-jnp.inf, lambda acc, s: jnp.maximum(
                acc, jnp.where(s < hi, s, -jnp.inf))), axis=1, keepdims=True)
            return a, b

        def open_rows(a, b):
            return jnp.max(jnp.where(a != b, 1.0, 0.0))

        lo, hi = lax.fori_loop(0, BISECT_STEPS, bisect, (lo, hi))
        a, b = bracket(lo, hi)

        def refine(state):
            lo, hi, a, b, _ = state
            ge = count(lambda s: s >= b) >= kf
            lo = jnp.where(ge, b, a)
            hi = jnp.where(ge, hi, b)
            lo, hi = lax.fori_loop(0, 4, bisect, (lo, hi))
            a, b = bracket(lo, hi)
            return lo, hi, a, b, open_rows(a, b)

        _, _, a, _, _ = lax.while_loop(lambda st: st[4] > 0, refine,
                                       (lo, hi, a, b, open_rows(a, b)))
        tau_ref[...] = a

        n_gt = count(lambda s: s > a)
        n_eq = count(lambda s: s == a)
        need = kf - n_gt

        @pl.when(jnp.max(jnp.where(n_eq > need, 1.0, 0.0)) > 0.0)
        def _():
            ra = lax.broadcasted_iota(jnp.int32, (KEY_CHUNK, KEY_CHUNK), 0)
            rb = lax.broadcasted_iota(jnp.int32, (KEY_CHUNK, KEY_CHUNK), 1)
            before = jnp.where(ra < rb, 1.0, 0.0).astype(bf16)

            def drop(j, seen):
                k0 = pl.multiple_of(j * KEY_CHUNK, KEY_CHUNK)
                s = s_ref[:, pl.ds(k0, KEY_CHUNK)]
                eq = jnp.where(s == a, 1.0, 0.0)
                rank = jnp.dot(eq.astype(bf16), before, preferred_element_type=f32) + seen
                s_ref[:, pl.ds(k0, KEY_CHUNK)] = jnp.where(
                    s == a, jnp.where(rank >= need, -jnp.inf, s), s)
                return seen + jnp.sum(eq, axis=1, keepdims=True)

            lax.fori_loop(0, nchunk, drop, jnp.zeros((Q_TILE, 1), f32))

    tau = tau_ref[...]
    q = jnp.concatenate([qabs_ref[:, h * KV_LATENT:(h + 1) * KV_LATENT]
                         for h in range(N_ATT_HEADS)], axis=0)
    rows_all = N_ATT_HEADS * Q_TILE

    def attend(j, mla):
        m, l, acc = mla
        k0 = pl.multiple_of(j * KEY_CHUNK, KEY_CHUNK)
        sc = jnp.dot(q, ct_ref[:, pl.ds(k0, KEY_CHUNK)], preferred_element_type=f32)
        keep = s_ref[:, pl.ds(k0, KEY_CHUNK)] >= tau
        keep = jnp.broadcast_to(keep[None], (N_ATT_HEADS, Q_TILE, KEY_CHUNK))
        sc = jnp.where(keep, sc.reshape(N_ATT_HEADS, Q_TILE, KEY_CHUNK), NEG_BIG)
        sc = sc.reshape(rows_all, KEY_CHUNK)
        m_new = jnp.maximum(m, jnp.max(sc, axis=1, keepdims=True))
        scale = jnp.exp(m - m_new)
        p = jnp.exp(sc - m_new)
        l = scale * l + jnp.sum(p, axis=1, keepdims=True)
        acc = scale * acc + jnp.dot(p.astype(bf16), c_ref[pl.ds(k0, KEY_CHUNK), :],
                                    preferred_element_type=f32)
        return m_new, l, acc

    m0 = jnp.full((rows_all, 1), NEG_BIG, f32)
    l0 = jnp.zeros((rows_all, 1), f32)
    a0 = jnp.zeros((rows_all, KV_LATENT), f32)
    _, l, acc = lax.fori_loop(0, nchunk, attend, (m0, l0, a0))
    o_lat = (acc / l).astype(bf16)

    for j in range(N_ATT_HEADS // 2):
        pair = jnp.concatenate([o_lat[2 * j * Q_TILE:(2 * j + 1) * Q_TILE],
                                o_lat[(2 * j + 1) * Q_TILE:(2 * j + 2) * Q_TILE]], axis=1)
        o_ref[:, j * LANES:(j + 1) * LANES] = jnp.dot(
            pair, wuv_ref[j], preferred_element_type=f32).astype(bf16)


def _attn(qabs, qi, kw, ct, c, kit, wuv2, batch, seq, topk):
    T = batch * seq
    nq = seq // Q_TILE

    def qrow(n):
        return pl.BlockSpec((Q_TILE, n), lambda b, i: (b * nq + i, 0))

    return pl.pallas_call(
        functools.partial(_attn_kernel, topk),
        grid=(batch, nq),
        in_specs=[qrow(N_ATT_HEADS * KV_LATENT), qrow(IDX_HEADS * IDX_DIM), qrow(LANES),
                  pl.BlockSpec((KV_LATENT, seq), lambda b, i: (0, b)),
                  pl.BlockSpec((seq, KV_LATENT), lambda b, i: (b, 0)),
                  pl.BlockSpec((IDX_DIM, seq), lambda b, i: (0, b)),
                  pl.BlockSpec(wuv2.shape, lambda b, i: (0, 0, 0), pipeline_mode=pl.Buffered(1))],
        out_specs=qrow(ATT_WIDTH),
        out_shape=jax.ShapeDtypeStruct((T, ATT_WIDTH), bf16),
        scratch_shapes=[pltpu.VMEM((Q_TILE, seq), f32), pltpu.VMEM((Q_TILE, 1), f32)],
        compiler_params=pltpu.CompilerParams(dimension_semantics=("arbitrary", "arbitrary"),
                                             vmem_limit_bytes=VMEM_LIMIT),
        name="attn",
    )(qabs, qi, kw, ct, c, kit, wuv2)


def _out_ffn_ln_kernel(alpha, h_ref, att_ref, gm_ref, wo_ref, g2_ref, b2_ref,
                       w1_ref, w3_ref, w2_ref, g3_ref, b3_ref, o_ref):
    mix = jnp.concatenate([att_ref[...], gm_ref[...]], axis=1)
    m = jnp.dot(mix, wo_ref[...], preferred_element_type=f32)
    x = _layernorm(alpha * h_ref[...] + m, g2_ref[...], b2_ref[...])
    y = alpha * x + 0.5 * _swiglu(x.astype(bf16), w1_ref, w3_ref, w2_ref)
    o_ref[...] = _layernorm(y, g3_ref[...], b3_ref[...])


def _out_ffn_ln(h, att, gm, wo, g2, b2, w1, w3, w2, g3, b3, alpha):
    T = h.shape[0]
    row = pl.BlockSpec((FFN_TILE, D_MODEL), lambda i: (i, 0))
    half = pl.BlockSpec((FFN_TILE, ATT_WIDTH), lambda i: (i, 0))
    consts = [wo, g2, b2, w1, w3, w2, g3, b3]
    return pl.pallas_call(
        functools.partial(_out_ffn_ln_kernel, alpha),
        grid=(T // FFN_TILE,),
        in_specs=[row, half, half] + [_const_spec(a.shape) for a in consts],
        out_specs=row,
        out_shape=jax.ShapeDtypeStruct((T, D_MODEL), f32),
        compiler_params=pltpu.CompilerParams(dimension_semantics=("arbitrary",),
                                             vmem_limit_bytes=VMEM_LIMIT),
        name="out_ffn_ln",
    )(h, att, gm, *consts)


def _pack_w_in(w_in):
    splits = np.cumsum([ATT_WIDTH, KV_LATENT, IDX_HEADS * IDX_DIM, IDX_DIM, IDX_HEADS]).tolist()
    q, c, qi, ki, wi, z = jnp.split(w_in, splits, axis=-1)
    pad = jnp.zeros((w_in.shape[0], LANES - IDX_DIM - IDX_HEADS), w_in.dtype)
    return jnp.concatenate([q, c, qi, ki, wi, pad, z], axis=-1).astype(bf16)


def _pair_blockdiag(w):
    H, r, c = w.shape
    z = jnp.zeros((H // 2, r, c), w.dtype)
    top = jnp.concatenate([w[0::2], z], axis=2)
    bot = jnp.concatenate([z, w[1::2]], axis=2)
    return jnp.concatenate([top, bot], axis=1)


def kernel(x, ffn1_w1, ffn1_w3, ffn1_w2, ln1_g, ln1_b, w_in, kv_norm_g, w_uk, w_uv, gmlp_ln_g, gmlp_ln_b, gmlp_ws, gmlp_bs, w_out, ln2_g, ln2_b, ffn2_w1, ffn2_w3, ffn2_w2, ln3_g, ln3_b):
    B, L, D = x.shape
    depth = ffn1_w1.shape[0]
    alpha = float((2 * depth) ** 0.25)
    topk = min(TOPK_MAX, L // 4)
    assert D == D_MODEL and L % KEY_CHUNK == 0 and (B * L) % FFN_TILE == 0
    assert topk % Q_TILE == 0 and KEY_CHUNK % Q_TILE == 0 and PROJ_TILE % CHUNK == 0

    def vec(a):
        return a.reshape(1, -1)

    h = x.reshape(B * L, D)
    for l in range(depth):
        h = _ffn_ln(h, ffn1_w1[l].astype(bf16), ffn1_w3[l].astype(bf16), ffn1_w2[l].astype(bf16),
                    vec(ln1_g[l]), vec(ln1_b[l]), alpha)
        wuk2 = _pair_blockdiag(jnp.swapaxes(w_uk[l], 1, 2)).astype(bf16)
        wuv2 = _pair_blockdiag(w_uv[l]).astype(bf16)
        bs_full = jnp.repeat(gmlp_bs[l].T, MLP_GROUP_DIM, axis=1)
        qabs, c, ct, qi, kit, kw, gm = _proj(
            h, _pack_w_in(w_in[l]), vec(kv_norm_g[l]), wuk2, vec(gmlp_ln_g[l]), vec(gmlp_ln_b[l]),
            gmlp_ws[l], bs_full)
        att = _attn(qabs, qi, kw, ct, c, kit, wuv2, B, L, topk)
        h = _out_ffn_ln(h, att, gm, w_out[l].astype(bf16), vec(ln2_g[l]), vec(ln2_b[l]),
                        ffn2_w1[l].astype(bf16), ffn2_w3[l].astype(bf16), ffn2_w2[l].astype(bf16),
                        vec(ln3_g[l]), vec(ln3_b[l]), alpha)
    return h.reshape(B, L, D)
```

```python
import functools

import jax
import jax.numpy as jnp
import numpy as np
from jax import lax
from jax.experimental import pallas as pl
from jax.experimental.pallas import tpu as pltpu

D_MODEL = 1024
N_ATT_HEADS = 8
ATT_HEAD_DIM = 64
ATT_WIDTH = N_ATT_HEADS * ATT_HEAD_DIM
KV_LATENT = 128
IDX_HEADS = 4
IDX_DIM = 64
TOPK_MAX = 256
N_MLP_GROUPS = 8
MLP_GROUP_DIM = 64
MLP_WIDTH = N_MLP_GROUPS * MLP_GROUP_DIM
CHUNK = 128
D_FF = 2816
LN_EPS = 1e-5

LANES = 128
SUBLANES = 8
Q_TILE = 128
IDX_CHUNK = 512
KV_CHUNK = 256
FOLD_ROWS = 64
FFN_TILE = 512
FF_CHUNK = 1408
PROJ_TILE = 256
BISECT_STEPS = 22
REFINE_STEPS = 3
VMEM_LIMIT = 56 * 1024 * 1024
NEG_BIG = -1e30
LOG2E = 1.4426950408889634
SUM_ROWS = 16

_PC_Q = 0
_PC_C = ATT_WIDTH
_PC_QI = _PC_C + KV_LATENT
_PC_KW = _PC_QI + IDX_HEADS * IDX_DIM
_PC_Z = _PC_KW + LANES
_PC_END = _PC_Z + 2 * MLP_WIDTH

bf16 = jnp.bfloat16
f32 = jnp.float32


def _const_spec(shape):
    nd = len(shape)
    return pl.BlockSpec(shape, lambda *_: (0,) * nd, pipeline_mode=pl.Buffered(1))


def _layernorm(y, g, b):
    mu = jnp.mean(y, axis=-1, keepdims=True)
    d = y - mu
    var = jnp.mean(d * d, axis=-1, keepdims=True)
    return d * lax.rsqrt(var + LN_EPS) * g + b


def _swiglu(xb, w1_ref, w3_ref, w2_ref):
    acc = None
    for c0 in range(0, D_FF, FF_CHUNK):
        a = jnp.dot(xb, w1_ref[:, c0:c0 + FF_CHUNK], preferred_element_type=f32)
        b = jnp.dot(xb, w3_ref[:, c0:c0 + FF_CHUNK], preferred_element_type=f32)
        h = (jax.nn.silu(a) * b).astype(bf16)
        part = jnp.dot(h, w2_ref[c0:c0 + FF_CHUNK, :], preferred_element_type=f32)
        acc = part if acc is None else acc + part
    return acc


def _ffn_ln_kernel(alpha, x_ref, w1_ref, w3_ref, w2_ref, g_ref, b_ref, o_ref):
    x = x_ref[...]
    y = alpha * x + 0.5 * _swiglu(x.astype(bf16), w1_ref, w3_ref, w2_ref)
    o_ref[...] = _layernorm(y, g_ref[...], b_ref[...])


def _ffn_ln(x, w1, w3, w2, g, b, alpha):
    T = x.shape[0]
    row = pl.BlockSpec((FFN_TILE, D_MODEL), lambda i: (i, 0))
    return pl.pallas_call(
        functools.partial(_ffn_ln_kernel, alpha),
        grid=(T // FFN_TILE,),
        in_specs=[row, _const_spec(w1.shape), _const_spec(w3.shape), _const_spec(w2.shape),
                  _const_spec(g.shape), _const_spec(b.shape)],
        out_specs=row,
        out_shape=jax.ShapeDtypeStruct((T, D_MODEL), f32),
        compiler_params=pltpu.CompilerParams(dimension_semantics=("arbitrary",),
                                             vmem_limit_bytes=VMEM_LIMIT),
        name="ffn_ln",
    )(x, w1, w3, w2, g, b)


def _proj_kernel(h_ref, wp_ref, kvg_ref, wuk_ref, glng_ref, glnb_ref, ws_ref, bs_ref,
                 qabst_ref, c_ref, ct_ref, qit_ref, ki_ref, wit_ref, gm_ref):
    hb = h_ref[...].astype(bf16)
    proj = jnp.dot(hb, wp_ref[...], preferred_element_type=f32)

    for j in range(N_ATT_HEADS // 2):
        qj = proj[:, _PC_Q + j * LANES:_PC_Q + (j + 1) * LANES].astype(bf16)
        qa = jnp.dot(qj, wuk_ref[j], preferred_element_type=f32) * (ATT_HEAD_DIM ** -0.5 * LOG2E)
        qabst_ref[2 * j * KV_LATENT:2 * (j + 1) * KV_LATENT, :] = qa.T.astype(bf16)

    c = proj[:, _PC_C:_PC_C + KV_LATENT]
    c = c * lax.rsqrt(jnp.mean(c * c, axis=-1, keepdims=True) + LN_EPS) * kvg_ref[...]
    c_ref[...] = c.astype(bf16)
    ones_row = lax.broadcasted_iota(jnp.int32, (SUM_ROWS, PROJ_TILE), 0) == 0
    ct_ref[...] = jnp.concatenate([c.T, jnp.where(ones_row, 1.0, 0.0)], axis=0).astype(bf16)

    qit_ref[...] = proj[:, _PC_QI:_PC_QI + IDX_HEADS * IDX_DIM].T.astype(bf16)
    kw = proj[:, _PC_KW:_PC_KW + LANES]
    ki_ref[...] = kw[:, :IDX_DIM].astype(bf16)
    wit_ref[...] = kw.T[IDX_DIM:IDX_DIM + SUBLANES, :]

    z = jax.nn.gelu(proj[:, _PC_Z:_PC_END])
    u = z[:, :MLP_WIDTH]
    v = _layernorm(z[:, MLP_WIDTH:], glng_ref[...], glnb_ref[...]).astype(bf16)
    r = lax.broadcasted_iota(jnp.int32, (CHUNK, CHUNK), 0)
    s = lax.broadcasted_iota(jnp.int32, (CHUNK, CHUNK), 1)
    tri = s <= r
    left = lax.broadcasted_iota(jnp.int32, (CHUNK, LANES), 1) < MLP_GROUP_DIM
    for t0 in range(0, PROJ_TILE, CHUNK):
        for j in range(N_MLP_GROUPS // 2):
            vj = v[t0:t0 + CHUNK, j * LANES:(j + 1) * LANES]
            wa = jnp.where(tri, ws_ref[2 * j], 0.0).astype(bf16)
            wb = jnp.where(tri, ws_ref[2 * j + 1], 0.0).astype(bf16)
            sa = jnp.dot(wa, vj, preferred_element_type=f32)
            sb = jnp.dot(wb, vj, preferred_element_type=f32)
            sv = jnp.where(left, sa, sb) + bs_ref[:, j * LANES:(j + 1) * LANES]
            gm_ref[t0:t0 + CHUNK, j * LANES:(j + 1) * LANES] = (
                u[t0:t0 + CHUNK, j * LANES:(j + 1) * LANES] * sv).astype(bf16)


def _proj(h, wp, kvg, wuk2, glng, glnb, ws, bs_full):
    T = h.shape[0]
    tm = PROJ_TILE

    def rows(n, dt):
        return pl.BlockSpec((tm, n), lambda i: (i, 0)), jax.ShapeDtypeStruct((T, n), dt)

    def cols(n, dt):
        return pl.BlockSpec((n, tm), lambda i: (0, i)), jax.ShapeDtypeStruct((n, T), dt)

    outs = [cols(N_ATT_HEADS * KV_LATENT, bf16),
            rows(KV_LATENT, bf16),
            cols(KV_LATENT + SUM_ROWS, bf16),
            cols(IDX_HEADS * IDX_DIM, bf16),
            rows(IDX_DIM, bf16),
            cols(SUBLANES, f32),
            rows(MLP_WIDTH, bf16)]
    return pl.pallas_call(
        _proj_kernel,
        grid=(T // tm,),
        in_specs=[pl.BlockSpec((tm, D_MODEL), lambda i: (i, 0)),
                  _const_spec(wp.shape), _const_spec(kvg.shape), _const_spec(wuk2.shape),
                  _const_spec(glng.shape), _const_spec(glnb.shape), _const_spec(ws.shape),
                  _const_spec(bs_full.shape)],
        out_specs=[o[0] for o in outs],
        out_shape=[o[1] for o in outs],
        compiler_params=pltpu.CompilerParams(dimension_semantics=("arbitrary",),
                                             vmem_limit_bytes=VMEM_LIMIT),
        name="proj",
    )(h, wp, kvg, wuk2, glng, glnb, ws, bs_full)


def _attn_kernel(topk, qabst_ref, qit_ref, wit_ref, ki_ref, c_ref, ct_ref, wuvt_ref, o_ref,
                 st_ref, tau_ref, sca_ref, scb_ref, acc_ref, m_ref, mx_ref):
    i = pl.program_id(1)
    n_idx = i // (IDX_CHUNK // Q_TILE) + 1
    n_kv = i // (KV_CHUNK // Q_TILE) + 1
    qpos = i * Q_TILE + lax.broadcasted_iota(jnp.int32, (1, Q_TILE), 1)
    kf = float(topk)
    fold_shape = (FOLD_ROWS, Q_TILE)

    def fold_init(v):
        return jnp.full(fold_shape, v, f32)

    def fold_blocks(acc, s, fn):
        for t in range(0, s.shape[0], FOLD_ROWS):
            acc = fn(acc, s[t:t + FOLD_ROWS])
        return acc

    qit = qit_ref[...]
    qcat = jnp.concatenate([qit[h * IDX_DIM:(h + 1) * IDX_DIM, :] for h in range(IDX_HEADS)],
                           axis=1)
    wit = wit_ref[...] * (IDX_HEADS ** -0.5 * IDX_DIM ** -0.5)
    wh = [wit[h:h + 1, :] for h in range(IDX_HEADS)]

    def score_chunk(j, carry):
        mn, mx = carry
        k0 = pl.multiple_of(j * IDX_CHUNK, IDX_CHUNK)
        lg = jnp.dot(ki_ref[pl.ds(k0, IDX_CHUNK), :], qcat, preferred_element_type=f32)
        s = None
        for h in range(IDX_HEADS):
            term = wh[h] * jnp.maximum(lg[:, h * Q_TILE:(h + 1) * Q_TILE], 0.0)
            s = term if s is None else s + term
        kpos = k0 + lax.broadcasted_iota(jnp.int32, (IDX_CHUNK, 1), 0)
        causal = kpos <= qpos
        st_ref[pl.ds(k0, IDX_CHUNK), :] = jnp.where(causal, s, -jnp.inf)
        mn = fold_blocks(mn, jnp.where(causal, s, jnp.inf), jnp.minimum)
        mx = fold_blocks(mx, jnp.where(causal, s, -jnp.inf), jnp.maximum)
        return mn, mx

    mn, mx = lax.fori_loop(0, n_idx, score_chunk, (fold_init(jnp.inf), fold_init(-jnp.inf)))

    def fold_keys(inits, fn):
        def body(j, accs):
            k0 = pl.multiple_of(j * IDX_CHUNK, IDX_CHUNK)
            for t in range(0, IDX_CHUNK, FOLD_ROWS):
                accs = fn(accs, st_ref[pl.ds(k0 + t, FOLD_ROWS), :])
            return accs
        return lax.fori_loop(0, n_idx, body, tuple(fold_init(v) for v in inits))

    def count_ge(v):
        vb = jnp.broadcast_to(v, fold_shape)
        acc, = fold_keys((0.0,), lambda a, s: (a[0] + jnp.where(s >= vb, 1.0, 0.0),))
        return jnp.sum(acc, axis=0, keepdims=True)

    @pl.when(i * Q_TILE + Q_TILE <= topk)
    def _():
        tau_ref[...] = jnp.full((1, Q_TILE), -jnp.inf, f32)

    @pl.when(i * Q_TILE + Q_TILE > topk)
    def _():
        lo = jnp.min(mn, axis=0, keepdims=True)
        top = jnp.max(mx, axis=0, keepdims=True)
        hi = top + jnp.maximum(jnp.abs(top) * 1e-3, 1e-30)

        def bisect(_, lh):
            lo, hi = lh
            mid = 0.5 * lo + 0.5 * hi
            ge = count_ge(mid) >= kf
            return jnp.where(ge, mid, lo), jnp.where(ge, hi, mid)

        def candidate(lo):
            lob = jnp.broadcast_to(lo, fold_shape)
            a, = fold_keys((jnp.inf,), lambda a, s: (
                jnp.minimum(a[0], jnp.where(s >= lob, s, jnp.inf)),))
            tau = jnp.min(a, axis=0, keepdims=True)
            taub = jnp.broadcast_to(tau, fold_shape)
            n_gt, n_eq = fold_keys((0.0, 0.0), lambda g, s: (
                g[0] + jnp.where(s > taub, 1.0, 0.0), g[1] + jnp.where(s == taub, 1.0, 0.0)))
            n_gt = jnp.sum(n_gt, axis=0, keepdims=True)
            n_eq = jnp.sum(n_eq, axis=0, keepdims=True)
            return tau, n_gt, n_eq, jnp.max(jnp.where(n_gt >= kf, 1.0, 0.0))

        lo, hi = lax.fori_loop(0, BISECT_STEPS, bisect, (lo, hi))

        def refine(state):
            _, hi, tau, n_gt, _, _ = state
            taub = jnp.broadcast_to(tau, fold_shape)
            nxt, = fold_keys((jnp.inf,), lambda a, s: (
                jnp.minimum(a[0], jnp.where(s > taub, s, jnp.inf)),))
            lo = jnp.where(n_gt >= kf, jnp.min(nxt, axis=0, keepdims=True), tau)
            lo, hi = lax.fori_loop(0, REFINE_STEPS, bisect, (lo, hi))
            return (lo, hi) + candidate(lo)

        _, _, tau, n_gt, n_eq, _ = lax.while_loop(lambda st: st[5] > 0.0, refine,
                                                  (lo, hi) + candidate(lo))
        tau_ref[...] = tau

        need = kf - n_gt
        surplus = n_eq > need

        @pl.when(jnp.max(jnp.where(surplus, 1.0, 0.0)) > 0.0)
        def _():
            ra = lax.broadcasted_iota(jnp.int32, (IDX_CHUNK, IDX_CHUNK), 0)
            rb = lax.broadcasted_iota(jnp.int32, (IDX_CHUNK, IDX_CHUNK), 1)
            earlier = jnp.where(rb < ra, 1.0, 0.0).astype(bf16)

            def drop(j, seen):
                k0 = pl.multiple_of(j * IDX_CHUNK, IDX_CHUNK)
                s = st_ref[pl.ds(k0, IDX_CHUNK), :]
                eq = jnp.where(s == tau, 1.0, 0.0)
                rank = jnp.dot(earlier, eq.astype(bf16), preferred_element_type=f32) + seen
                st_ref[pl.ds(k0, IDX_CHUNK), :] = jnp.where(
                    s == tau, jnp.where(rank >= need, -jnp.inf, s), s)
                return seen + jnp.sum(eq, axis=0, keepdims=True)

            lax.fori_loop(0, n_idx, drop, jnp.zeros((1, Q_TILE), f32))

    tau = tau_ref[...]
    qabst = qabst_ref[...]
    q_all = jnp.concatenate([qabst[h * KV_LATENT:(h + 1) * KV_LATENT, :]
                             for h in range(N_ATT_HEADS)], axis=1)

    def stage1(dst, j):
        k0 = pl.multiple_of(j * KV_CHUNK, KV_CHUNK)
        s = st_ref[pl.ds(k0, KV_CHUNK), :]
        kpos = k0 + lax.broadcasted_iota(jnp.int32, (KV_CHUNK, 1), 0)
        bias = jnp.where(kpos <= qpos, jnp.where(s >= tau, 0.0, NEG_BIG), NEG_BIG)
        sc = jnp.dot(c_ref[pl.ds(k0, KV_CHUNK), :], q_all, preferred_element_type=f32)
        sc = sc + jnp.concatenate([bias] * N_ATT_HEADS, axis=1)
        dst[...] = sc
        mx_ref[...] = jnp.max(sc, axis=0, keepdims=True)

    def stage2(src, j, mx):
        k0 = pl.multiple_of(j * KV_CHUNK, KV_CHUNK)
        m_old = m_ref[...]
        m_new = jnp.maximum(m_old, mx)
        scale = jnp.exp2(m_old - m_new)
        p = jnp.exp2(src[...] - m_new)
        m_ref[...] = m_new
        pv = jnp.dot(ct_ref[:, pl.ds(k0, KV_CHUNK)], p.astype(bf16), preferred_element_type=f32)
        acc_ref[...] = scale * acc_ref[...] + pv

    def step(src, dst, t):
        mx = mx_ref[...]
        stage1(dst, t + 1)
        stage2(src, t, mx)

    m_ref[...] = jnp.full(m_ref.shape, NEG_BIG, f32)
    acc_ref[...] = jnp.zeros(acc_ref.shape, f32)
    stage1(sca_ref, 0)
    n_steps = n_kv - 1

    def two_steps(u, carry):
        step(sca_ref, scb_ref, 2 * u)
        step(scb_ref, sca_ref, 2 * u + 1)
        return carry

    lax.fori_loop(0, n_steps // 2, two_steps, 0)

    @pl.when(n_steps % 2 == 1)
    def _():
        step(sca_ref, scb_ref, n_steps - 1)
        stage2(scb_ref, n_steps, mx_ref[...])

    @pl.when(n_steps % 2 == 0)
    def _():
        stage2(sca_ref, n_steps, mx_ref[...])

    o_lat = acc_ref[:KV_LATENT, :] / acc_ref[KV_LATENT:KV_LATENT + 1, :]
    for g in range(N_ATT_HEADS // 2):
        pair = jnp.concatenate([o_lat[:, 2 * g * Q_TILE:(2 * g + 1) * Q_TILE],
                                o_lat[:, (2 * g + 1) * Q_TILE:(2 * g + 2) * Q_TILE]],
                               axis=0).astype(bf16)
        att_t = jnp.dot(wuvt_ref[g], pair, preferred_element_type=f32)
        o_ref[:, g * LANES:(g + 1) * LANES] = att_t.T.astype(bf16)


def _attn(qabst, qit, wit, ki, c, ct, wuvt2, batch, seq, topk):
    T = batch * seq
    nq = seq // Q_TILE

    def qcol(n):
        return pl.BlockSpec((n, Q_TILE), lambda b, i: (0, b * nq + i))

    return pl.pallas_call(
        functools.partial(_attn_kernel, topk),
        grid=(batch, nq),
        in_specs=[qcol(N_ATT_HEADS * KV_LATENT), qcol(IDX_HEADS * IDX_DIM), qcol(SUBLANES),
                  pl.BlockSpec((seq, IDX_DIM), lambda b, i: (b, 0)),
                  pl.BlockSpec((seq, KV_LATENT), lambda b, i: (b, 0)),
                  pl.BlockSpec((KV_LATENT + SUM_ROWS, seq), lambda b, i: (0, b)),
                  pl.BlockSpec(wuvt2.shape, lambda b, i: (0, 0, 0), pipeline_mode=pl.Buffered(1))],
        out_specs=pl.BlockSpec((Q_TILE, ATT_WIDTH), lambda b, i: (b * nq + i, 0)),
        out_shape=jax.ShapeDtypeStruct((T, ATT_WIDTH), bf16),
        scratch_shapes=[pltpu.VMEM((seq, Q_TILE), f32),
                        pltpu.VMEM((1, Q_TILE), f32),
                        pltpu.VMEM((KV_CHUNK, N_ATT_HEADS * Q_TILE), f32),
                        pltpu.VMEM((KV_CHUNK, N_ATT_HEADS * Q_TILE), f32),
                        pltpu.VMEM((KV_LATENT + SUM_ROWS, N_ATT_HEADS * Q_TILE), f32),
                        pltpu.VMEM((1, N_ATT_HEADS * Q_TILE), f32),
                        pltpu.VMEM((1, N_ATT_HEADS * Q_TILE), f32)],
        compiler_params=pltpu.CompilerParams(dimension_semantics=("arbitrary", "arbitrary"),
                                             vmem_limit_bytes=VMEM_LIMIT),
        name="attn",
    )(qabst, qit, wit, ki, c, ct, wuvt2)


def _out_ffn_ln_kernel(alpha, h_ref, att_ref, gm_ref, wo_ref, g2_ref, b2_ref,
                       w1_ref, w3_ref, w2_ref, g3_ref, b3_ref, o_ref):
    mix = jnp.concatenate([att_ref[...], gm_ref[...]], axis=1)
    m = jnp.dot(mix, wo_ref[...], preferred_element_type=f32)
    x = _layernorm(alpha * h_ref[...] + m, g2_ref[...], b2_ref[...])
    y = alpha * x + 0.5 * _swiglu(x.astype(bf16), w1_ref, w3_ref, w2_ref)
    o_ref[...] = _layernorm(y, g3_ref[...], b3_ref[...])


def _out_ffn_ln(h, att, gm, wo, g2, b2, w1, w3, w2, g3, b3, alpha):
    T = h.shape[0]
    row = pl.BlockSpec((FFN_TILE, D_MODEL), lambda i: (i, 0))
    half = pl.BlockSpec((FFN_TILE, ATT_WIDTH), lambda i: (i, 0))
    consts = [wo, g2, b2, w1, w3, w2, g3, b3]
    return pl.pallas_call(
        functools.partial(_out_ffn_ln_kernel, alpha),
        grid=(T // FFN_TILE,),
        in_specs=[row, half, half] + [_const_spec(a.shape) for a in consts],
        out_specs=row,
        out_shape=jax.ShapeDtypeStruct((T, D_MODEL), f32),
        compiler_params=pltpu.CompilerParams(dimension_semantics=("arbitrary",),
                                             vmem_limit_bytes=VMEM_LIMIT),
        name="out_ffn_ln",
    )(h, att, gm, *consts)


def _pack_w_in(w_in):
    splits = np.cumsum([ATT_WIDTH, KV_LATENT, IDX_HEADS * IDX_DIM, IDX_DIM, IDX_HEADS]).tolist()
    q, c, qi, ki, wi, z = jnp.split(w_in, splits, axis=-1)
    pad = jnp.zeros((w_in.shape[0], LANES - IDX_DIM - IDX_HEADS), w_in.dtype)
    return jnp.concatenate([q, c, qi, ki, wi, pad, z], axis=-1).astype(bf16)


def _pair_blockdiag(w):
    H, r, c = w.shape
    z = jnp.zeros((H // 2, r, c), w.dtype)
    top = jnp.concatenate([w[0::2], z], axis=2)
    bot = jnp.concatenate([z, w[1::2]], axis=2)
    return jnp.concatenate([top, bot], axis=1)


def kernel(x, ffn1_w1, ffn1_w3, ffn1_w2, ln1_g, ln1_b, w_in, kv_norm_g, w_uk, w_uv, gmlp_ln_g, gmlp_ln_b, gmlp_ws, gmlp_bs, w_out, ln2_g, ln2_b, ffn2_w1, ffn2_w3, ffn2_w2, ln3_g, ln3_b):
    B, L, D = x.shape
    depth = ffn1_w1.shape[0]
    alpha = float((2 * depth) ** 0.25)
    topk = min(TOPK_MAX, L // 4)
    assert D == D_MODEL and L % IDX_CHUNK == 0 and (B * L) % FFN_TILE == 0
    assert topk % Q_TILE == 0 and IDX_CHUNK % KV_CHUNK == 0 and KV_CHUNK % Q_TILE == 0
    assert PROJ_TILE % CHUNK == 0 and IDX_CHUNK % FOLD_ROWS == 0

    def vec(a):
        return a.reshape(1, -1)

    h = x.reshape(B * L, D)
    for l in range(depth):
        h = _ffn_ln(h, ffn1_w1[l].astype(bf16), ffn1_w3[l].astype(bf16), ffn1_w2[l].astype(bf16),
                    vec(ln1_g[l]), vec(ln1_b[l]), alpha)
        wuk2 = _pair_blockdiag(jnp.swapaxes(w_uk[l], 1, 2)).astype(bf16)
        wuvt2 = _pair_blockdiag(jnp.swapaxes(w_uv[l], 1, 2)).astype(bf16)
        bs_full = jnp.repeat(gmlp_bs[l].T, MLP_GROUP_DIM, axis=1)
        qabst, c, ct, qit, ki, wit, gm = _proj(
            h, _pack_w_in(w_in[l]), vec(kv_norm_g[l]), wuk2, vec(gmlp_ln_g[l]), vec(gmlp_ln_b[l]),
            gmlp_ws[l], bs_full)
        att = _attn(qabst, qit, wit, ki, c, ct, wuvt2, B, L, topk)
        h = _out_ffn_ln(h, att, gm, w_out[l].astype(bf16), vec(ln2_g[l]), vec(ln2_b[l]),
                        ffn2_w1[l].astype(bf16), ffn2_w3[l].astype(bf16), ffn2_w2[l].astype(bf16),
                        vec(ln3_g[l]), vec(ln3_b[l]), alpha)
    return h.reshape(B, L, D)
```

```python
import functools

import jax
import jax.numpy as jnp
import numpy as np
from jax import lax
from jax.experimental import pallas as pl
from jax.experimental.pallas import tpu as pltpu

D_MODEL = 1024
N_ATT_HEADS = 8
ATT_HEAD_DIM = 64
ATT_WIDTH = N_ATT_HEADS * ATT_HEAD_DIM
KV_LATENT = 128
IDX_HEADS = 4
IDX_DIM = 64
TOPK_MAX = 256
N_MLP_GROUPS = 8
MLP_GROUP_DIM = 64
MLP_WIDTH = N_MLP_GROUPS * MLP_GROUP_DIM
CHUNK = 128
D_FF = 2816
LN_EPS = 1e-5

LANES = 128
SUBLANES = 8
Q_TILE = 128
IDX_CHUNK = 512
KV_CHUNK = 256
FOLD_ROWS = 64
FFN_TILE = 512
FF_CHUNK = 2816
PROJ_TILE = 512
BISECT_STEPS = 22
REFINE_STEPS = 3
VMEM_LIMIT = 56 * 1024 * 1024
NEG_BIG = -(2.0 ** 100)
LOG2E = 1.4426950408889634
SUM_ROWS = 16

_PC_Q = 0
_PC_C = ATT_WIDTH
_PC_QI = _PC_C + KV_LATENT
_PC_KW = _PC_QI + IDX_HEADS * IDX_DIM
_PC_Z = _PC_KW + LANES
_PC_END = _PC_Z + 2 * MLP_WIDTH

bf16 = jnp.bfloat16
f32 = jnp.float32


def _const_spec(shape):
    nd = len(shape)
    return pl.BlockSpec(shape, lambda *_: (0,) * nd, pipeline_mode=pl.Buffered(1))


def _layernorm(y, g, b):
    mu = jnp.mean(y, axis=-1, keepdims=True)
    d = y - mu
    var = jnp.mean(d * d, axis=-1, keepdims=True)
    return d * lax.rsqrt(var + LN_EPS) * g + b


def _swiglu(xb, w1_ref, w3_ref, w2_ref):
    acc = None
    for c0 in range(0, D_FF, FF_CHUNK):
        a = jnp.dot(xb, w1_ref[:, c0:c0 + FF_CHUNK], preferred_element_type=f32)
        b = jnp.dot(xb, w3_ref[:, c0:c0 + FF_CHUNK], preferred_element_type=f32)
        h = (jax.nn.silu(a) * b).astype(bf16)
        part = jnp.dot(h, w2_ref[c0:c0 + FF_CHUNK, :], preferred_element_type=f32)
        acc = part if acc is None else acc + part
    return acc


def _ffn_ln_kernel(alpha, x_ref, w1_ref, w3_ref, w2_ref, g_ref, b_ref, o_ref):
    x = x_ref[...]
    y = alpha * x + 0.5 * _swiglu(x.astype(bf16), w1_ref, w3_ref, w2_ref)
    o_ref[...] = _layernorm(y, g_ref[...], b_ref[...])


def _ffn_ln(x, w1, w3, w2, g, b, alpha):
    T = x.shape[0]
    row = pl.BlockSpec((FFN_TILE, D_MODEL), lambda i: (i, 0))
    return pl.pallas_call(
        functools.partial(_ffn_ln_kernel, alpha),
        grid=(T // FFN_TILE,),
        in_specs=[row, _const_spec(w1.shape), _const_spec(w3.shape), _const_spec(w2.shape),
                  _const_spec(g.shape), _const_spec(b.shape)],
        out_specs=row,
        out_shape=jax.ShapeDtypeStruct((T, D_MODEL), f32),
        compiler_params=pltpu.CompilerParams(dimension_semantics=("arbitrary",),
                                             vmem_limit_bytes=VMEM_LIMIT),
        name="ffn_ln",
    )(x, w1, w3, w2, g, b)


def _proj_kernel(h_ref, wp_ref, kvg_ref, wuk_ref, glng_ref, glnb_ref, ws_ref, bs_ref,
                 qabst_ref, c_ref, ct_ref, qit_ref, ki_ref, wit_ref, gm_ref):
    hb = h_ref[...].astype(bf16)
    proj = jnp.dot(hb, wp_ref[...], preferred_element_type=f32)

    for j in range(N_ATT_HEADS // 2):
        qj = proj[:, _PC_Q + j * LANES:_PC_Q + (j + 1) * LANES].astype(bf16)
        qa = jnp.dot(qj, wuk_ref[j], preferred_element_type=f32) * (ATT_HEAD_DIM ** -0.5 * LOG2E)
        qabst_ref[2 * j * KV_LATENT:2 * (j + 1) * KV_LATENT, :] = qa.T.astype(bf16)

    c = proj[:, _PC_C:_PC_C + KV_LATENT]
    c = c * lax.rsqrt(jnp.mean(c * c, axis=-1, keepdims=True) + LN_EPS) * kvg_ref[...]
    c_ref[...] = c.astype(bf16)
    ones_row = lax.broadcasted_iota(jnp.int32, (SUM_ROWS, PROJ_TILE), 0) == 0
    ct_ref[...] = jnp.concatenate([c.T, jnp.where(ones_row, 1.0, 0.0)], axis=0).astype(bf16)

    qit_ref[...] = proj[:, _PC_QI:_PC_QI + IDX_HEADS * IDX_DIM].T.astype(bf16)
    kw = proj[:, _PC_KW:_PC_KW + LANES]
    ki_ref[...] = kw[:, :IDX_DIM].astype(bf16)
    wit_ref[...] = kw.T[IDX_DIM:IDX_DIM + SUBLANES, :]

    z = jax.nn.gelu(proj[:, _PC_Z:_PC_END])
    u = z[:, :MLP_WIDTH]
    v = _layernorm(z[:, MLP_WIDTH:], glng_ref[...], glnb_ref[...]).astype(bf16)
    r = lax.broadcasted_iota(jnp.int32, (CHUNK, CHUNK), 0)
    s = lax.broadcasted_iota(jnp.int32, (CHUNK, CHUNK), 1)
    tri = s <= r
    left = lax.broadcasted_iota(jnp.int32, (CHUNK, LANES), 1) < MLP_GROUP_DIM
    for t0 in range(0, PROJ_TILE, CHUNK):
        for j in range(N_MLP_GROUPS // 2):
            vj = v[t0:t0 + CHUNK, j * LANES:(j + 1) * LANES]
            wa = jnp.where(tri, ws_ref[2 * j], 0.0).astype(bf16)
            wb = jnp.where(tri, ws_ref[2 * j + 1], 0.0).astype(bf16)
            sa = jnp.dot(wa, vj, preferred_element_type=f32)
            sb = jnp.dot(wb, vj, preferred_element_type=f32)
            sv = jnp.where(left, sa, sb) + bs_ref[:, j * LANES:(j + 1) * LANES]
            gm_ref[t0:t0 + CHUNK, j * LANES:(j + 1) * LANES] = (
                u[t0:t0 + CHUNK, j * LANES:(j + 1) * LANES] * sv).astype(bf16)


def _proj(h, wp, kvg, wuk2, glng, glnb, ws, bs_full):
    T = h.shape[0]
    tm = PROJ_TILE

    def rows(n, dt):
        return pl.BlockSpec((tm, n), lambda i: (i, 0)), jax.ShapeDtypeStruct((T, n), dt)

    def cols(n, dt):
        return pl.BlockSpec((n, tm), lambda i: (0, i)), jax.ShapeDtypeStruct((n, T), dt)

    outs = [cols(N_ATT_HEADS * KV_LATENT, bf16),
            rows(KV_LATENT, bf16),
            cols(KV_LATENT + SUM_ROWS, bf16),
            cols(IDX_HEADS * IDX_DIM, bf16),
            rows(IDX_DIM, bf16),
            cols(SUBLANES, f32),
            rows(MLP_WIDTH, bf16)]
    return pl.pallas_call(
        _proj_kernel,
        grid=(T // tm,),
        in_specs=[pl.BlockSpec((tm, D_MODEL), lambda i: (i, 0)),
                  _const_spec(wp.shape), _const_spec(kvg.shape), _const_spec(wuk2.shape),
                  _const_spec(glng.shape), _const_spec(glnb.shape), _const_spec(ws.shape),
                  _const_spec(bs_full.shape)],
        out_specs=[o[0] for o in outs],
        out_shape=[o[1] for o in outs],
        compiler_params=pltpu.CompilerParams(dimension_semantics=("arbitrary",),
                                             vmem_limit_bytes=VMEM_LIMIT),
        name="proj",
    )(h, wp, kvg, wuk2, glng, glnb, ws, bs_full)


def _attn_kernel(topk, qabst_ref, qit_ref, wit_ref, ki_ref, c_ref, ct_ref, wuvt_ref, o_ref,
                 st_ref, tau_ref, sca_ref, scb_ref, acc_ref, m_ref, mx_ref):
    i = pl.program_id(1)
    n_idx = i // (IDX_CHUNK // Q_TILE) + 1
    n_kv = i // (KV_CHUNK // Q_TILE) + 1
    qpos = i * Q_TILE + lax.broadcasted_iota(jnp.int32, (1, Q_TILE), 1)
    kf = float(topk)
    fold_shape = (FOLD_ROWS, Q_TILE)

    def fold_init(v):
        return jnp.full(fold_shape, v, f32)

    def fold_blocks(acc, s, fn):
        for t in range(0, s.shape[0], FOLD_ROWS):
            acc = fn(acc, s[t:t + FOLD_ROWS])
        return acc

    qit = qit_ref[...]
    qcat = jnp.concatenate([qit[h * IDX_DIM:(h + 1) * IDX_DIM, :] for h in range(IDX_HEADS)],
                           axis=1)
    wit = wit_ref[...] * (IDX_HEADS ** -0.5 * IDX_DIM ** -0.5)
    wh = [wit[h:h + 1, :] for h in range(IDX_HEADS)]

    def scores(j):
        k0 = pl.multiple_of(j * IDX_CHUNK, IDX_CHUNK)
        lg = jnp.dot(ki_ref[pl.ds(k0, IDX_CHUNK), :], qcat, preferred_element_type=f32)
        s = None
        for h in range(IDX_HEADS):
            term = wh[h] * jnp.maximum(lg[:, h * Q_TILE:(h + 1) * Q_TILE], 0.0)
            s = term if s is None else s + term
        return k0, s

    def causal_chunk(j, carry):
        mn, mx = carry
        k0, s = scores(j)
        st_ref[pl.ds(k0, IDX_CHUNK), :] = s
        return fold_blocks(mn, s, jnp.minimum), fold_blocks(mx, s, jnp.maximum)

    def causal_pair(u, carry):
        return causal_chunk(2 * u + 1, causal_chunk(2 * u, carry))

    n_full = n_idx - 1
    mn, mx = lax.fori_loop(0, n_full // 2, causal_pair, (fold_init(jnp.inf), fold_init(-jnp.inf)))
    mn, mx = lax.cond(n_full % 2 == 1, lambda c: causal_chunk(n_full - 1, c), lambda c: c, (mn, mx))
    k0, s = scores(n_idx - 1)
    causal = k0 + lax.broadcasted_iota(jnp.int32, (IDX_CHUNK, 1), 0) <= qpos
    st_ref[pl.ds(k0, IDX_CHUNK), :] = jnp.where(causal, s, -jnp.inf)
    mn = fold_blocks(mn, jnp.where(causal, s, jnp.inf), jnp.minimum)
    mx = fold_blocks(mx, jnp.where(causal, s, -jnp.inf), jnp.maximum)

    def fold_keys(inits, fn):
        def body(j, accs):
            k0 = pl.multiple_of(j * IDX_CHUNK, IDX_CHUNK)
            for t in range(0, IDX_CHUNK, FOLD_ROWS):
                accs = fn(accs, st_ref[pl.ds(k0 + t, FOLD_ROWS), :])
            return accs
        return lax.fori_loop(0, n_idx, body, tuple(fold_init(v) for v in inits))

    def count_ge(v):
        vb = jnp.broadcast_to(v, fold_shape)
        acc, = fold_keys((0.0,), lambda a, s: (a[0] + jnp.where(s >= vb, 1.0, 0.0),))
        return jnp.sum(acc, axis=0, keepdims=True)

    @pl.when(i * Q_TILE + Q_TILE <= topk)
    def _():
        tau_ref[...] = jnp.full((1, Q_TILE), -jnp.inf, f32)

    @pl.when(i * Q_TILE + Q_TILE > topk)
    def _():
        lo = jnp.min(mn, axis=0, keepdims=True)
        top = jnp.max(mx, axis=0, keepdims=True)
        hi = top + jnp.maximum(jnp.abs(top) * 1e-3, 1e-30)

        def bisect(_, lh):
            lo, hi = lh
            mid = 0.5 * lo + 0.5 * hi
            ge = count_ge(mid) >= kf
            return jnp.where(ge, mid, lo), jnp.where(ge, hi, mid)

        def candidate(lo):
            lob = jnp.broadcast_to(lo, fold_shape)
            a, = fold_keys((jnp.inf,), lambda a, s: (
                jnp.minimum(a[0], jnp.where(s >= lob, s, jnp.inf)),))
            tau = jnp.min(a, axis=0, keepdims=True)
            taub = jnp.broadcast_to(tau, fold_shape)
            n_gt, n_eq = fold_keys((0.0, 0.0), lambda g, s: (
                g[0] + jnp.where(s > taub, 1.0, 0.0), g[1] + jnp.where(s == taub, 1.0, 0.0)))
            n_gt = jnp.sum(n_gt, axis=0, keepdims=True)
            n_eq = jnp.sum(n_eq, axis=0, keepdims=True)
            return tau, n_gt, n_eq, jnp.max(jnp.where(n_gt >= kf, 1.0, 0.0))

        lo, hi = lax.fori_loop(0, BISECT_STEPS, bisect, (lo, hi))

        def refine(state):
            _, hi, tau, n_gt, _, _ = state
            taub = jnp.broadcast_to(tau, fold_shape)
            nxt, = fold_keys((jnp.inf,), lambda a, s: (
                jnp.minimum(a[0], jnp.where(s > taub, s, jnp.inf)),))
            lo = jnp.where(n_gt >= kf, jnp.min(nxt, axis=0, keepdims=True), tau)
            lo, hi = lax.fori_loop(0, REFINE_STEPS, bisect, (lo, hi))
            return (lo, hi) + candidate(lo)

        _, _, tau, n_gt, n_eq, _ = lax.while_loop(lambda st: st[5] > 0.0, refine,
                                                  (lo, hi) + candidate(lo))
        tau_ref[...] = tau

        need = kf - n_gt
        surplus = n_eq > need

        @pl.when(jnp.max(jnp.where(surplus, 1.0, 0.0)) > 0.0)
        def _():
            ra = lax.broadcasted_iota(jnp.int32, (IDX_CHUNK, IDX_CHUNK), 0)
            rb = lax.broadcasted_iota(jnp.int32, (IDX_CHUNK, IDX_CHUNK), 1)
            earlier = jnp.where(rb < ra, 1.0, 0.0).astype(bf16)

            def drop(j, seen):
                k0 = pl.multiple_of(j * IDX_CHUNK, IDX_CHUNK)
                s = st_ref[pl.ds(k0, IDX_CHUNK), :]
                eq = jnp.where(s == tau, 1.0, 0.0)
                rank = jnp.dot(earlier, eq.astype(bf16), preferred_element_type=f32) + seen
                st_ref[pl.ds(k0, IDX_CHUNK), :] = jnp.where(
                    s == tau, jnp.where(rank >= need, -jnp.inf, s), s)
                return seen + jnp.sum(eq, axis=0, keepdims=True)

            lax.fori_loop(0, n_idx, drop, jnp.zeros((1, Q_TILE), f32))

    tau = tau_ref[...]
    qabst = qabst_ref[...]
    q_all = jnp.concatenate([qabst[h * KV_LATENT:(h + 1) * KV_LATENT, :]
                             for h in range(N_ATT_HEADS)], axis=1)
    eye = (lax.broadcasted_iota(jnp.int32, (Q_TILE, Q_TILE), 0)
           == lax.broadcasted_iota(jnp.int32, (Q_TILE, Q_TILE), 1))
    neg_eye = jnp.where(eye, NEG_BIG, 0.0).astype(bf16)
    q_aug = jnp.concatenate([q_all, jnp.concatenate([neg_eye] * N_ATT_HEADS, axis=1)], axis=0)

    def stage1(dst, j):
        k0 = pl.multiple_of(j * KV_CHUNK, KV_CHUNK)
        s = st_ref[pl.ds(k0, KV_CHUNK), :]
        kpos = k0 + lax.broadcasted_iota(jnp.int32, (KV_CHUNK, 1), 0)
        dropped = jnp.where(kpos <= qpos, jnp.where(s >= tau, 0.0, 1.0), 1.0).astype(bf16)
        lhs = jnp.concatenate([c_ref[pl.ds(k0, KV_CHUNK), :], dropped], axis=1)
        sc = jnp.dot(lhs, q_aug, preferred_element_type=f32)
        dst[...] = sc
        mx_ref[...] = jnp.max(sc, axis=0, keepdims=True)

    def stage2(src, j, mx):
        k0 = pl.multiple_of(j * KV_CHUNK, KV_CHUNK)
        m_old = m_ref[...]
        m_new = jnp.maximum(m_old, mx)
        scale = jnp.exp2(m_old - m_new)
        p = jnp.exp2(src[...] - m_new)
        m_ref[...] = m_new
        pv = jnp.dot(ct_ref[:, pl.ds(k0, KV_CHUNK)], p.astype(bf16), preferred_element_type=f32)
        acc_ref[...] = scale * acc_ref[...] + pv

    def step(src, dst, t):
        mx = mx_ref[...]
        stage1(dst, t + 1)
        stage2(src, t, mx)

    m_ref[...] = jnp.full(m_ref.shape, NEG_BIG, f32)
    acc_ref[...] = jnp.zeros(acc_ref.shape, f32)
    stage1(sca_ref, 0)
    n_steps = n_kv - 1

    def two_steps(u, carry):
        step(sca_ref, scb_ref, 2 * u)
        step(scb_ref, sca_ref, 2 * u + 1)
        return carry

    lax.fori_loop(0, n_steps // 2, two_steps, 0)

    @pl.when(n_steps % 2 == 1)
    def _():
        step(sca_ref, scb_ref, n_steps - 1)
        stage2(scb_ref, n_steps, mx_ref[...])

    @pl.when(n_steps % 2 == 0)
    def _():
        stage2(sca_ref, n_steps, mx_ref[...])

    o_lat = acc_ref[:KV_LATENT, :] / acc_ref[KV_LATENT:KV_LATENT + 1, :]
    for g in range(N_ATT_HEADS // 2):
        pair = jnp.concatenate([o_lat[:, 2 * g * Q_TILE:(2 * g + 1) * Q_TILE],
                                o_lat[:, (2 * g + 1) * Q_TILE:(2 * g + 2) * Q_TILE]],
                               axis=0).astype(bf16)
        att_t = jnp.dot(wuvt_ref[g], pair, preferred_element_type=f32)
        o_ref[:, g * LANES:(g + 1) * LANES] = att_t.T.astype(bf16)


def _attn(qabst, qit, wit, ki, c, ct, wuvt2, batch, seq, topk):
    T = batch * seq
    nq = seq // Q_TILE

    def qcol(n):
        return pl.BlockSpec((n, Q_TILE), lambda b, i: (0, b * nq + i))

    return pl.pallas_call(
        functools.partial(_attn_kernel, topk),
        grid=(batch, nq),
        in_specs=[qcol(N_ATT_HEADS * KV_LATENT), qcol(IDX_HEADS * IDX_DIM), qcol(SUBLANES),
                  pl.BlockSpec((seq, IDX_DIM), lambda b, i: (b, 0)),
                  pl.BlockSpec((seq, KV_LATENT), lambda b, i: (b, 0)),
                  pl.BlockSpec((KV_LATENT + SUM_ROWS, seq), lambda b, i: (0, b)),
                  pl.BlockSpec(wuvt2.shape, lambda b, i: (0, 0, 0), pipeline_mode=pl.Buffered(1))],
        out_specs=pl.BlockSpec((Q_TILE, ATT_WIDTH), lambda b, i: (b * nq + i, 0)),
        out_shape=jax.ShapeDtypeStruct((T, ATT_WIDTH), bf16),
        scratch_shapes=[pltpu.VMEM((seq, Q_TILE), f32),
                        pltpu.VMEM((1, Q_TILE), f32),
                        pltpu.VMEM((KV_CHUNK, N_ATT_HEADS * Q_TILE), f32),
                        pltpu.VMEM((KV_CHUNK, N_ATT_HEADS * Q_TILE), f32),
                        pltpu.VMEM((KV_LATENT + SUM_ROWS, N_ATT_HEADS * Q_TILE), f32),
                        pltpu.VMEM((1, N_ATT_HEADS * Q_TILE), f32),
                        pltpu.VMEM((1, N_ATT_HEADS * Q_TILE), f32)],
        compiler_params=pltpu.CompilerParams(dimension_semantics=("arbitrary", "arbitrary"),
                                             vmem_limit_bytes=VMEM_LIMIT),
        name="attn",
    )(qabst, qit, wit, ki, c, ct, wuvt2)


def _out_ffn_ln_kernel(alpha, h_ref, att_ref, gm_ref, wo_ref, g2_ref, b2_ref,
                       w1_ref, w3_ref, w2_ref, g3_ref, b3_ref, o_ref):
    mix = jnp.concatenate([att_ref[...], gm_ref[...]], axis=1)
    m = jnp.dot(mix, wo_ref[...], preferred_element_type=f32)
    x = _layernorm(alpha * h_ref[...] + m, g2_ref[...], b2_ref[...])
    y = alpha * x + 0.5 * _swiglu(x.astype(bf16), w1_ref, w3_ref, w2_ref)
    o_ref[...] = _layernorm(y, g3_ref[...], b3_ref[...])


def _out_ffn_ln(h, att, gm, wo, g2, b2, w1, w3, w2, g3, b3, alpha):
    T = h.shape[0]
    row = pl.BlockSpec((FFN_TILE, D_MODEL), lambda i: (i, 0))
    half = pl.BlockSpec((FFN_TILE, ATT_WIDTH), lambda i: (i, 0))
    consts = [wo, g2, b2, w1, w3, w2, g3, b3]
    return pl.pallas_call(
        functools.partial(_out_ffn_ln_kernel, alpha),
        grid=(T // FFN_TILE,),
        in_specs=[row, half, half] + [_const_spec(a.shape) for a in consts],
        out_specs=row,
        out_shape=jax.ShapeDtypeStruct((T, D_MODEL), f32),
        compiler_params=pltpu.CompilerParams(dimension_semantics=("arbitrary",),
                                             vmem_limit_bytes=VMEM_LIMIT),
        name="out_ffn_ln",
    )(h, att, gm, *consts)


def _pack_w_in(w_in):
    splits = np.cumsum([ATT_WIDTH, KV_LATENT, IDX_HEADS * IDX_DIM, IDX_DIM, IDX_HEADS]).tolist()
    q, c, qi, ki, wi, z = jnp.split(w_in, splits, axis=-1)
    pad = jnp.zeros((w_in.shape[0], LANES - IDX_DIM - IDX_HEADS), w_in.dtype)
    return jnp.concatenate([q, c, qi, ki, wi, pad, z], axis=-1).astype(bf16)


def _pair_blockdiag(w):
    H, r, c = w.shape
    z = jnp.zeros((H // 2, r, c), w.dtype)
    top = jnp.concatenate([w[0::2], z], axis=2)
    bot = jnp.concatenate([z, w[1::2]], axis=2)
    return jnp.concatenate([top, bot], axis=1)


def kernel(x, ffn1_w1, ffn1_w3, ffn1_w2, ln1_g, ln1_b, w_in, kv_norm_g, w_uk, w_uv, gmlp_ln_g, gmlp_ln_b, gmlp_ws, gmlp_bs, w_out, ln2_g, ln2_b, ffn2_w1, ffn2_w3, ffn2_w2, ln3_g, ln3_b):
    B, L, D = x.shape
    depth = ffn1_w1.shape[0]
    alpha = float((2 * depth) ** 0.25)
    topk = min(TOPK_MAX, L // 4)
    assert D == D_MODEL and L % IDX_CHUNK == 0 and (B * L) % FFN_TILE == 0
    assert topk % Q_TILE == 0 and IDX_CHUNK % KV_CHUNK == 0 and KV_CHUNK % Q_TILE == 0
    assert PROJ_TILE % CHUNK == 0 and IDX_CHUNK % FOLD_ROWS == 0

    def vec(a):
        return a.reshape(1, -1)

    h = x.reshape(B * L, D)
    for l in range(depth):
        h = _ffn_ln(h, ffn1_w1[l].astype(bf16), ffn1_w3[l].astype(bf16), ffn1_w2[l].astype(bf16),
                    vec(ln1_g[l]), vec(ln1_b[l]), alpha)
        wuk2 = _pair_blockdiag(jnp.swapaxes(w_uk[l], 1, 2)).astype(bf16)
        wuvt2 = _pair_blockdiag(jnp.swapaxes(w_uv[l], 1, 2)).astype(bf16)
        bs_full = jnp.repeat(gmlp_bs[l].T, MLP_GROUP_DIM, axis=1)
        qabst, c, ct, qit, ki, wit, gm = _proj(
            h, _pack_w_in(w_in[l]), vec(kv_norm_g[l]), wuk2, vec(gmlp_ln_g[l]), vec(gmlp_ln_b[l]),
            gmlp_ws[l], bs_full)
        att = _attn(qabst, qit, wit, ki, c, ct, wuvt2, B, L, topk)
        h = _out_ffn_ln(h, att, gm, w_out[l].astype(bf16), vec(ln2_g[l]), vec(ln2_b[l]),
                        ffn2_w1[l].astype(bf16), ffn2_w3[l].astype(bf16), ffn2_w2[l].astype(bf16),
                        vec(ln3_g[l]), vec(ln3_b[l]), alpha)
    return h.reshape(B, L, D)
```

```python
import functools

import jax
import jax.numpy as jnp
import numpy as np
from jax import lax
from jax.experimental import pallas as pl
from jax.experimental.pallas import tpu as pltpu

D_MODEL = 1024
N_ATT_HEADS = 8
ATT_HEAD_DIM = 64
ATT_WIDTH = N_ATT_HEADS * ATT_HEAD_DIM
KV_LATENT = 128
IDX_HEADS = 4
IDX_DIM = 64
TOPK_MAX = 256
N_MLP_GROUPS = 8
MLP_GROUP_DIM = 64
MLP_WIDTH = N_MLP_GROUPS * MLP_GROUP_DIM
CHUNK = 128
D_FF = 2816
LN_EPS = 1e-5

LANES = 128
SUBLANES = 8
Q_TILE = 128
IDX_CHUNK = 512
KV_CHUNK = 256
FOLD_ROWS = 64
FFN_TILE = 512
FF_CHUNK = 2816
PROJ_TILE = 512
BISECT16_STEPS = 10
BISECT_STEPS = 12
FOLD16_ROWS = 128
REFINE_STEPS = 3
VMEM_LIMIT = 56 * 1024 * 1024
NEG_BIG = -(2.0 ** 100)
LOG2E = 1.4426950408889634
SUM_ROWS = 16

_PC_Q = 0
_PC_C = ATT_WIDTH
_PC_QI = _PC_C + KV_LATENT
_PC_KW = _PC_QI + IDX_HEADS * IDX_DIM
_PC_Z = _PC_KW + LANES
_PC_END = _PC_Z + 2 * MLP_WIDTH

bf16 = jnp.bfloat16
f32 = jnp.float32


def _const_spec(shape):
    nd = len(shape)
    return pl.BlockSpec(shape, lambda *_: (0,) * nd, pipeline_mode=pl.Buffered(1))


def _layernorm(y, g, b):
    mu = jnp.mean(y, axis=-1, keepdims=True)
    d = y - mu
    var = jnp.mean(d * d, axis=-1, keepdims=True)
    return d * lax.rsqrt(var + LN_EPS) * g + b


def _swiglu(xb, w1_ref, w3_ref, w2_ref):
    acc = None
    for c0 in range(0, D_FF, FF_CHUNK):
        a = jnp.dot(xb, w1_ref[:, c0:c0 + FF_CHUNK], preferred_element_type=f32)
        b = jnp.dot(xb, w3_ref[:, c0:c0 + FF_CHUNK], preferred_element_type=f32)
        h = (jax.nn.silu(a) * b).astype(bf16)
        part = jnp.dot(h, w2_ref[c0:c0 + FF_CHUNK, :], preferred_element_type=f32)
        acc = part if acc is None else acc + part
    return acc


def _ffn_ln_kernel(alpha, x_ref, w1_ref, w3_ref, w2_ref, g_ref, b_ref, o_ref):
    x = x_ref[...]
    y = alpha * x + 0.5 * _swiglu(x.astype(bf16), w1_ref, w3_ref, w2_ref)
    o_ref[...] = _layernorm(y, g_ref[...], b_ref[...])


def _ffn_ln(x, w1, w3, w2, g, b, alpha):
    T = x.shape[0]
    row = pl.BlockSpec((FFN_TILE, D_MODEL), lambda i: (i, 0))
    return pl.pallas_call(
        functools.partial(_ffn_ln_kernel, alpha),
        grid=(T // FFN_TILE,),
        in_specs=[row, _const_spec(w1.shape), _const_spec(w3.shape), _const_spec(w2.shape),
                  _const_spec(g.shape), _const_spec(b.shape)],
        out_specs=row,
        out_shape=jax.ShapeDtypeStruct((T, D_MODEL), f32),
        compiler_params=pltpu.CompilerParams(dimension_semantics=("arbitrary",),
                                             vmem_limit_bytes=VMEM_LIMIT),
        name="ffn_ln",
    )(x, w1, w3, w2, g, b)


def _proj_kernel(h_ref, wp_ref, kvg_ref, wuk_ref, glng_ref, glnb_ref, ws_ref, bs_ref,
                 qabst_ref, c_ref, ct_ref, qit_ref, ki_ref, wit_ref, gm_ref):
    hb = h_ref[...].astype(bf16)
    proj = jnp.dot(hb, wp_ref[...], preferred_element_type=f32)

    for j in range(N_ATT_HEADS // 2):
        qj = proj[:, _PC_Q + j * LANES:_PC_Q + (j + 1) * LANES].astype(bf16)
        qa = jnp.dot(qj, wuk_ref[j], preferred_element_type=f32) * (ATT_HEAD_DIM ** -0.5 * LOG2E)
        for t in range(PROJ_TILE // Q_TILE):
            qabst_ref[t, 2 * j * KV_LATENT:2 * (j + 1) * KV_LATENT, :] = (
                qa[t * Q_TILE:(t + 1) * Q_TILE, :].T.astype(bf16))

    c = proj[:, _PC_C:_PC_C + KV_LATENT]
    c = c * lax.rsqrt(jnp.mean(c * c, axis=-1, keepdims=True) + LN_EPS) * kvg_ref[...]
    c_ref[...] = c.astype(bf16)
    ones_row = lax.broadcasted_iota(jnp.int32, (SUM_ROWS, PROJ_TILE), 0) == 0
    ct_ref[...] = jnp.concatenate([c.T, jnp.where(ones_row, 1.0, 0.0)], axis=0).astype(bf16)

    qi = proj[:, _PC_QI:_PC_QI + IDX_HEADS * IDX_DIM]
    kw = proj[:, _PC_KW:_PC_KW + LANES]
    ki_ref[...] = kw[:, :IDX_DIM].astype(bf16)
    for t in range(PROJ_TILE // Q_TILE):
        qit_ref[t] = qi[t * Q_TILE:(t + 1) * Q_TILE, :].T.astype(bf16)
        wit_ref[t] = kw[t * Q_TILE:(t + 1) * Q_TILE, :].T[IDX_DIM:IDX_DIM + SUBLANES, :]

    z = jax.nn.gelu(proj[:, _PC_Z:_PC_END])
    u = z[:, :MLP_WIDTH]
    v = _layernorm(z[:, MLP_WIDTH:], glng_ref[...], glnb_ref[...]).astype(bf16)
    r = lax.broadcasted_iota(jnp.int32, (CHUNK, CHUNK), 0)
    s = lax.broadcasted_iota(jnp.int32, (CHUNK, CHUNK), 1)
    tri = s <= r
    left = lax.broadcasted_iota(jnp.int32, (CHUNK, LANES), 1) < MLP_GROUP_DIM
    for t0 in range(0, PROJ_TILE, CHUNK):
        for j in range(N_MLP_GROUPS // 2):
            vj = v[t0:t0 + CHUNK, j * LANES:(j + 1) * LANES]
            wa = jnp.where(tri, ws_ref[2 * j], 0.0).astype(bf16)
            wb = jnp.where(tri, ws_ref[2 * j + 1], 0.0).astype(bf16)
            sa = jnp.dot(wa, vj, preferred_element_type=f32)
            sb = jnp.dot(wb, vj, preferred_element_type=f32)
            sv = jnp.where(left, sa, sb) + bs_ref[:, j * LANES:(j + 1) * LANES]
            gm_ref[t0:t0 + CHUNK, j * LANES:(j + 1) * LANES] = (
                u[t0:t0 + CHUNK, j * LANES:(j + 1) * LANES] * sv).astype(bf16)


def _proj(h, wp, kvg, wuk2, glng, glnb, ws, bs_full):
    T = h.shape[0]
    tm = PROJ_TILE

    def rows(n, dt):
        return pl.BlockSpec((tm, n), lambda i: (i, 0)), jax.ShapeDtypeStruct((T, n), dt)

    def cols(n, dt):
        return pl.BlockSpec((n, tm), lambda i: (0, i)), jax.ShapeDtypeStruct((n, T), dt)

    def qtiles(n, dt):
        return (pl.BlockSpec((tm // Q_TILE, n, Q_TILE), lambda i: (i, 0, 0)),
                jax.ShapeDtypeStruct((T // Q_TILE, n, Q_TILE), dt))

    outs = [qtiles(N_ATT_HEADS * KV_LATENT, bf16),
            rows(KV_LATENT, bf16),
            cols(KV_LATENT + SUM_ROWS, bf16),
            qtiles(IDX_HEADS * IDX_DIM, bf16),
            rows(IDX_DIM, bf16),
            qtiles(SUBLANES, f32),
            rows(MLP_WIDTH, bf16)]
    return pl.pallas_call(
        _proj_kernel,
        grid=(T // tm,),
        in_specs=[pl.BlockSpec((tm, D_MODEL), lambda i: (i, 0)),
                  _const_spec(wp.shape), _const_spec(kvg.shape), _const_spec(wuk2.shape),
                  _const_spec(glng.shape), _const_spec(glnb.shape), _const_spec(ws.shape),
                  _const_spec(bs_full.shape)],
        out_specs=[o[0] for o in outs],
        out_shape=[o[1] for o in outs],
        compiler_params=pltpu.CompilerParams(dimension_semantics=("arbitrary",),
                                             vmem_limit_bytes=VMEM_LIMIT),
        name="proj",
    )(h, wp, kvg, wuk2, glng, glnb, ws, bs_full)


def _attn_kernel(topk, qabst_ref, qit_ref, wit_ref, ki_ref, c_ref, ct_ref, wuvt_ref, o_ref,
                 st_ref, s16_ref, tau_ref, sca_ref, scb_ref, acc_ref, m_ref, mx_ref):
    i = pl.program_id(1)
    n_idx = i // (IDX_CHUNK // Q_TILE) + 1
    n_kv = i // (KV_CHUNK // Q_TILE) + 1
    qpos = i * Q_TILE + lax.broadcasted_iota(jnp.int32, (1, Q_TILE), 1)
    kf = float(topk)
    fold_shape = (FOLD_ROWS, Q_TILE)

    def fold_init(v):
        return jnp.full(fold_shape, v, f32)

    def fold_blocks(acc, s, fn):
        for t in range(0, s.shape[0], FOLD_ROWS):
            acc = fn(acc, s[t:t + FOLD_ROWS])
        return acc

    qit = qit_ref[...]
    qcat = jnp.concatenate([qit[h * IDX_DIM:(h + 1) * IDX_DIM, :] for h in range(IDX_HEADS)],
                           axis=1)
    wit = wit_ref[...] * (IDX_HEADS ** -0.5 * IDX_DIM ** -0.5)
    wh = [wit[h:h + 1, :] for h in range(IDX_HEADS)]

    def scores(j):
        k0 = pl.multiple_of(j * IDX_CHUNK, IDX_CHUNK)
        lg = jnp.dot(ki_ref[pl.ds(k0, IDX_CHUNK), :], qcat, preferred_element_type=f32)
        s = None
        for h in range(IDX_HEADS):
            term = wh[h] * jnp.maximum(lg[:, h * Q_TILE:(h + 1) * Q_TILE], 0.0)
            s = term if s is None else s + term
        return k0, s

    def causal_chunk(j, carry):
        mn, mx = carry
        k0, s = scores(j)
        st_ref[pl.ds(k0, IDX_CHUNK), :] = s
        s16_ref[pl.ds(k0, IDX_CHUNK), :] = s.astype(bf16)
        return fold_blocks(mn, s, jnp.minimum), fold_blocks(mx, s, jnp.maximum)

    def causal_pair(u, carry):
        return causal_chunk(2 * u + 1, causal_chunk(2 * u, carry))

    n_full = n_idx - 1
    mn, mx = lax.fori_loop(0, n_full // 2, causal_pair, (fold_init(jnp.inf), fold_init(-jnp.inf)))
    mn, mx = lax.cond(n_full % 2 == 1, lambda c: causal_chunk(n_full - 1, c), lambda c: c, (mn, mx))
    k0, s = scores(n_idx - 1)
    causal = k0 + lax.broadcasted_iota(jnp.int32, (IDX_CHUNK, 1), 0) <= qpos
    st_ref[pl.ds(k0, IDX_CHUNK), :] = jnp.where(causal, s, -jnp.inf)
    s16_ref[pl.ds(k0, IDX_CHUNK), :] = jnp.where(causal, s, -jnp.inf).astype(bf16)
    mn = fold_blocks(mn, jnp.where(causal, s, jnp.inf), jnp.minimum)
    mx = fold_blocks(mx, jnp.where(causal, s, -jnp.inf), jnp.maximum)

    def fold_keys(inits, fn):
        def body(j, accs):
            k0 = pl.multiple_of(j * IDX_CHUNK, IDX_CHUNK)
            for t in range(0, IDX_CHUNK, FOLD_ROWS):
                accs = fn(accs, st_ref[pl.ds(k0 + t, FOLD_ROWS), :])
            return accs
        return lax.fori_loop(0, n_idx, body, tuple(fold_init(v) for v in inits))

    def count_ge(v):
        vb = jnp.broadcast_to(v, fold_shape)
        acc, = fold_keys((0.0,), lambda a, s: (a[0] + jnp.where(s >= vb, 1.0, 0.0),))
        return jnp.sum(acc, axis=0, keepdims=True)

    @pl.when(i * Q_TILE + Q_TILE <= topk)
    def _():
        tau_ref[...] = jnp.full((1, Q_TILE), -jnp.inf, f32)

    @pl.when(i * Q_TILE + Q_TILE > topk)
    def _():
        lo = jnp.min(mn, axis=0, keepdims=True)
        top = jnp.max(mx, axis=0, keepdims=True)
        hi = top + jnp.maximum(jnp.abs(top) * 1e-3, 1e-30)

        def bisect(_, lh):
            lo, hi = lh
            mid = 0.5 * lo + 0.5 * hi
            ge = count_ge(mid) >= kf
            return jnp.where(ge, mid, lo), jnp.where(ge, hi, mid)

        def count16_ge(v):
            vb = jnp.broadcast_to(v.astype(bf16), (FOLD16_ROWS, Q_TILE))
            one = jnp.ones((FOLD16_ROWS, Q_TILE), bf16)
            zero = jnp.zeros((FOLD16_ROWS, Q_TILE), bf16)

            def body(j, acc):
                k0 = pl.multiple_of(j * IDX_CHUNK, IDX_CHUNK)
                for t in range(0, IDX_CHUNK, FOLD16_ROWS):
                    acc = acc + jnp.where(s16_ref[pl.ds(k0 + t, FOLD16_ROWS), :] >= vb, one, zero)
                return acc

            acc = lax.fori_loop(0, n_idx, body, zero)
            return jnp.sum(acc.astype(f32), axis=0, keepdims=True)

        def bisect16(_, lh):
            lo, hi = lh
            v = (0.5 * lo + 0.5 * hi).astype(bf16).astype(f32)
            ge = count16_ge(v) >= kf
            below = v - jnp.maximum(jnp.abs(v) * (2.0 ** -7), 1e-30)
            return (jnp.where(ge, jnp.maximum(lo, below), lo),
                    jnp.where(ge, hi, jnp.minimum(hi, v)))

        def candidate(lo):
            lob = jnp.broadcast_to(lo, fold_shape)
            a, = fold_keys((jnp.inf,), lambda a, s: (
                jnp.minimum(a[0], jnp.where(s >= lob, s, jnp.inf)),))
            tau = jnp.min(a, axis=0, keepdims=True)
            taub = jnp.broadcast_to(tau, fold_shape)
            n_gt, n_eq = fold_keys((0.0, 0.0), lambda g, s: (
                g[0] + jnp.where(s > taub, 1.0, 0.0), g[1] + jnp.where(s == taub, 1.0, 0.0)))
            n_gt = jnp.sum(n_gt, axis=0, keepdims=True)
            n_eq = jnp.sum(n_eq, axis=0, keepdims=True)
            return tau, n_gt, n_eq, jnp.max(jnp.where(n_gt >= kf, 1.0, 0.0))

        lo, hi = lax.fori_loop(0, BISECT16_STEPS, bisect16, (lo, hi))
        lo, hi = lax.fori_loop(0, BISECT_STEPS, bisect, (lo, hi))

        def refine(state):
            _, hi, tau, n_gt, _, _ = state
            taub = jnp.broadcast_to(tau, fold_shape)
            nxt, = fold_keys((jnp.inf,), lambda a, s: (
                jnp.minimum(a[0], jnp.where(s > taub, s, jnp.inf)),))
            lo = jnp.where(n_gt >= kf, jnp.min(nxt, axis=0, keepdims=True), tau)
            lo, hi = lax.fori_loop(0, REFINE_STEPS, bisect, (lo, hi))
            return (lo, hi) + candidate(lo)

        _, _, tau, n_gt, n_eq, _ = lax.while_loop(lambda st: st[5] > 0.0, refine,
                                                  (lo, hi) + candidate(lo))
        tau_ref[...] = tau

        need = kf - n_gt
        surplus = n_eq > need

        @pl.when(jnp.max(jnp.where(surplus, 1.0, 0.0)) > 0.0)
        def _():
            ra = lax.broadcasted_iota(jnp.int32, (IDX_CHUNK, IDX_CHUNK), 0)
            rb = lax.broadcasted_iota(jnp.int32, (IDX_CHUNK, IDX_CHUNK), 1)
            earlier = jnp.where(rb < ra, 1.0, 0.0).astype(bf16)

            def drop(j, seen):
                k0 = pl.multiple_of(j * IDX_CHUNK, IDX_CHUNK)
                s = st_ref[pl.ds(k0, IDX_CHUNK), :]
                eq = jnp.where(s == tau, 1.0, 0.0)
                rank = jnp.dot(earlier, eq.astype(bf16), preferred_element_type=f32) + seen
                st_ref[pl.ds(k0, IDX_CHUNK), :] = jnp.where(
                    s == tau, jnp.where(rank >= need, -jnp.inf, s), s)
                return seen + jnp.sum(eq, axis=0, keepdims=True)

            lax.fori_loop(0, n_idx, drop, jnp.zeros((1, Q_TILE), f32))

    tau = tau_ref[...]
    qabst = qabst_ref[...]
    q_all = jnp.concatenate([qabst[h * KV_LATENT:(h + 1) * KV_LATENT, :]
                             for h in range(N_ATT_HEADS)], axis=1)
    eye = (lax.broadcasted_iota(jnp.int32, (Q_TILE, Q_TILE), 0)
           == lax.broadcasted_iota(jnp.int32, (Q_TILE, Q_TILE), 1))
    neg_eye = jnp.where(eye, NEG_BIG, 0.0).astype(bf16)
    q_aug = jnp.concatenate([q_all, jnp.concatenate([neg_eye] * N_ATT_HEADS, axis=1)], axis=0)

    def stage1(dst, j):
        k0 = pl.multiple_of(j * KV_CHUNK, KV_CHUNK)
        s = st_ref[pl.ds(k0, KV_CHUNK), :]
        kpos = k0 + lax.broadcasted_iota(jnp.int32, (KV_CHUNK, 1), 0)
        dropped = jnp.where(kpos <= qpos, jnp.where(s >= tau, 0.0, 1.0), 1.0).astype(bf16)
        lhs = jnp.concatenate([c_ref[pl.ds(k0, KV_CHUNK), :], dropped], axis=1)
        sc = jnp.dot(lhs, q_aug, preferred_element_type=f32)
        dst[...] = sc
        mx_ref[...] = jnp.max(sc, axis=0, keepdims=True)

    def stage2(src, j, mx):
        k0 = pl.multiple_of(j * KV_CHUNK, KV_CHUNK)
        m_old = m_ref[...]
        m_new = jnp.maximum(m_old, mx)
        scale = jnp.exp2(m_old - m_new)
        p = jnp.exp2(src[...] - m_new)
        m_ref[...] = m_new
        pv = jnp.dot(ct_ref[:, pl.ds(k0, KV_CHUNK)], p.astype(bf16), preferred_element_type=f32)
        acc_ref[...] = scale * acc_ref[...] + pv

    def step(src, dst, t):
        mx = mx_ref[...]
        stage1(dst, t + 1)
        stage2(src, t, mx)

    m_ref[...] = jnp.full(m_ref.shape, NEG_BIG, f32)
    acc_ref[...] = jnp.zeros(acc_ref.shape, f32)
    stage1(sca_ref, 0)
    n_steps = n_kv - 1

    def two_steps(u, carry):
        step(sca_ref, scb_ref, 2 * u)
        step(scb_ref, sca_ref, 2 * u + 1)
        return carry

    lax.fori_loop(0, n_steps // 2, two_steps, 0)

    @pl.when(n_steps % 2 == 1)
    def _():
        step(sca_ref, scb_ref, n_steps - 1)
        stage2(scb_ref, n_steps, mx_ref[...])

    @pl.when(n_steps % 2 == 0)
    def _():
        stage2(sca_ref, n_steps, mx_ref[...])

    o_lat = acc_ref[:KV_LATENT, :] / acc_ref[KV_LATENT:KV_LATENT + 1, :]
    for g in range(N_ATT_HEADS // 2):
        pair = jnp.concatenate([o_lat[:, 2 * g * Q_TILE:(2 * g + 1) * Q_TILE],
                                o_lat[:, (2 * g + 1) * Q_TILE:(2 * g + 2) * Q_TILE]],
                               axis=0).astype(bf16)
        att_t = jnp.dot(wuvt_ref[g], pair, preferred_element_type=f32)
        o_ref[:, g * LANES:(g + 1) * LANES] = att_t.T.astype(bf16)


def _attn(qabst, qit, wit, ki, c, ct, wuvt2, batch, seq, topk):
    T = batch * seq
    nq = seq // Q_TILE

    def qcol(n):
        return pl.BlockSpec((None, n, Q_TILE), lambda b, i: (b * nq + i, 0, 0))

    return pl.pallas_call(
        functools.partial(_attn_kernel, topk),
        grid=(batch, nq),
        in_specs=[qcol(N_ATT_HEADS * KV_LATENT), qcol(IDX_HEADS * IDX_DIM), qcol(SUBLANES),
                  pl.BlockSpec((seq, IDX_DIM), lambda b, i: (b, 0)),
                  pl.BlockSpec((seq, KV_LATENT), lambda b, i: (b, 0)),
                  pl.BlockSpec((KV_LATENT + SUM_ROWS, seq), lambda b, i: (0, b)),
                  pl.BlockSpec(wuvt2.shape, lambda b, i: (0, 0, 0), pipeline_mode=pl.Buffered(1))],
        out_specs=pl.BlockSpec((Q_TILE, ATT_WIDTH), lambda b, i: (b * nq + i, 0)),
        out_shape=jax.ShapeDtypeStruct((T, ATT_WIDTH), bf16),
        scratch_shapes=[pltpu.VMEM((seq, Q_TILE), f32),
                        pltpu.VMEM((seq, Q_TILE), bf16),
                        pltpu.VMEM((1, Q_TILE), f32),
                        pltpu.VMEM((KV_CHUNK, N_ATT_HEADS * Q_TILE), f32),
                        pltpu.VMEM((KV_CHUNK, N_ATT_HEADS * Q_TILE), f32),
                        pltpu.VMEM((KV_LATENT + SUM_ROWS, N_ATT_HEADS * Q_TILE), f32),
                        pltpu.VMEM((1, N_ATT_HEADS * Q_TILE), f32),
                        pltpu.VMEM((1, N_ATT_HEADS * Q_TILE), f32)],
        compiler_params=pltpu.CompilerParams(dimension_semantics=("arbitrary", "arbitrary"),
                                             vmem_limit_bytes=VMEM_LIMIT),
        name="attn",
    )(qabst, qit, wit, ki, c, ct, wuvt2)


def _out_ffn_ln_kernel(alpha, h_ref, att_ref, gm_ref, wo_ref, g2_ref, b2_ref,
                       w1_ref, w3_ref, w2_ref, g3_ref, b3_ref, o_ref):
    mix = jnp.concatenate([att_ref[...], gm_ref[...]], axis=1)
    m = jnp.dot(mix, wo_ref[...], preferred_element_type=f32)
    x = _layernorm(alpha * h_ref[...] + m, g2_ref[...], b2_ref[...])
    y = alpha * x + 0.5 * _swiglu(x.astype(bf16), w1_ref, w3_ref, w2_ref)
    o_ref[...] = _layernorm(y, g3_ref[...], b3_ref[...])


def _out_ffn_ln(h, att, gm, wo, g2, b2, w1, w3, w2, g3, b3, alpha):
    T = h.shape[0]
    row = pl.BlockSpec((FFN_TILE, D_MODEL), lambda i: (i, 0))
    half = pl.BlockSpec((FFN_TILE, ATT_WIDTH), lambda i: (i, 0))
    consts = [wo, g2, b2, w1, w3, w2, g3, b3]
    return pl.pallas_call(
        functools.partial(_out_ffn_ln_kernel, alpha),
        grid=(T // FFN_TILE,),
        in_specs=[row, half, half] + [_const_spec(a.shape) for a in consts],
        out_specs=row,
        out_shape=jax.ShapeDtypeStruct((T, D_MODEL), f32),
        compiler_params=pltpu.CompilerParams(dimension_semantics=("arbitrary",),
                                             vmem_limit_bytes=VMEM_LIMIT),
        name="out_ffn_ln",
    )(h, att, gm, *consts)


def _pack_w_in(w_in):
    splits = np.cumsum([ATT_WIDTH, KV_LATENT, IDX_HEADS * IDX_DIM, IDX_DIM, IDX_HEADS]).tolist()
    q, c, qi, ki, wi, z = jnp.split(w_in, splits, axis=-1)
    pad = jnp.zeros((w_in.shape[0], LANES - IDX_DIM - IDX_HEADS), w_in.dtype)
    return jnp.concatenate([q, c, qi, ki, wi, pad, z], axis=-1).astype(bf16)


def _pair_blockdiag(w):
    H, r, c = w.shape
    z = jnp.zeros((H // 2, r, c), w.dtype)
    top = jnp.concatenate([w[0::2], z], axis=2)
    bot = jnp.concatenate([z, w[1::2]], axis=2)
    return jnp.concatenate([top, bot], axis=1)


def kernel(x, ffn1_w1, ffn1_w3, ffn1_w2, ln1_g, ln1_b, w_in, kv_norm_g, w_uk, w_uv, gmlp_ln_g, gmlp_ln_b, gmlp_ws, gmlp_bs, w_out, ln2_g, ln2_b, ffn2_w1, ffn2_w3, ffn2_w2, ln3_g, ln3_b):
    B, L, D = x.shape
    depth = ffn1_w1.shape[0]
    alpha = float((2 * depth) ** 0.25)
    topk = min(TOPK_MAX, L // 4)
    assert D == D_MODEL and L % IDX_CHUNK == 0 and (B * L) % FFN_TILE == 0
    assert topk % Q_TILE == 0 and IDX_CHUNK % KV_CHUNK == 0 and KV_CHUNK % Q_TILE == 0
    assert PROJ_TILE % CHUNK == 0 and IDX_CHUNK % FOLD_ROWS == 0

    def vec(a):
        return a.reshape(1, -1)

    h = x.reshape(B * L, D)
    for l in range(depth):
        h = _ffn_ln(h, ffn1_w1[l].astype(bf16), ffn1_w3[l].astype(bf16), ffn1_w2[l].astype(bf16),
                    vec(ln1_g[l]), vec(ln1_b[l]), alpha)
        wuk2 = _pair_blockdiag(jnp.swapaxes(w_uk[l], 1, 2)).astype(bf16)
        wuvt2 = _pair_blockdiag(jnp.swapaxes(w_uv[l], 1, 2)).astype(bf16)
        bs_full = jnp.repeat(gmlp_bs[l].T, MLP_GROUP_DIM, axis=1)
        qabst, c, ct, qit, ki, wit, gm = _proj(
            h, _pack_w_in(w_in[l]), vec(kv_norm_g[l]), wuk2, vec(gmlp_ln_g[l]), vec(gmlp_ln_b[l]),
            gmlp_ws[l], bs_full)
        att = _attn(qabst, qit, wit, ki, c, ct, wuvt2, B, L, topk)
        h = _out_ffn_ln(h, att, gm, w_out[l].astype(bf16), vec(ln2_g[l]), vec(ln2_b[l]),
                        ffn2_w1[l].astype(bf16), ffn2_w3[l].astype(bf16), ffn2_w2[l].astype(bf16),
                        vec(ln3_g[l]), vec(ln3_b[l]), alpha)
    return h.reshape(B, L, D)
```

```python
import functools

import jax
import jax.numpy as jnp
import numpy as np
from jax import lax
from jax.experimental import pallas as pl
from jax.experimental.pallas import tpu as pltpu

D_MODEL = 1024
N_ATT_HEADS = 8
ATT_HEAD_DIM = 64
ATT_WIDTH = N_ATT_HEADS * ATT_HEAD_DIM
KV_LATENT = 128
IDX_HEADS = 4
IDX_DIM = 64
TOPK_MAX = 256
N_MLP_GROUPS = 8
MLP_GROUP_DIM = 64
MLP_WIDTH = N_MLP_GROUPS * MLP_GROUP_DIM
CHUNK = 128
D_FF = 2816
LN_EPS = 1e-5

LANES = 128
SUBLANES = 8
Q_SUB = LANES
N_SUB = 2
Q_TILE = N_SUB * Q_SUB
N_COLS = N_ATT_HEADS * Q_TILE
IDX_CHUNK = 512
KV_CHUNK = 256
FOLD_ROWS = 64
FFN_TILE = 512
FFN_ROWS = 256
FF_CHUNK = 2816
PROJ_TILE = 512
BISECT_STEPS = 22
REFINE_STEPS = 3
VMEM_LIMIT = 56 * 1024 * 1024
NEG_BIG = -(2.0 ** 100)
LOG2E = 1.4426950408889634
SUM_ROWS = 16

_PC_Q = 0
_PC_C = ATT_WIDTH
_PC_QI = _PC_C + KV_LATENT
_PC_KW = _PC_QI + IDX_HEADS * IDX_DIM
_PC_Z = _PC_KW + LANES
_PC_END = _PC_Z + 2 * MLP_WIDTH

bf16 = jnp.bfloat16
f32 = jnp.float32


def _const_spec(shape):
    nd = len(shape)
    return pl.BlockSpec(shape, lambda *_: (0,) * nd, pipeline_mode=pl.Buffered(1))


def _layernorm(y, g, b):
    mu = jnp.mean(y, axis=-1, keepdims=True)
    d = y - mu
    var = jnp.mean(d * d, axis=-1, keepdims=True)
    return d * lax.rsqrt(var + LN_EPS) * g + b


def _swiglu(xb, w1_ref, w3_ref, w2_ref):
    acc = None
    for c0 in range(0, D_FF, FF_CHUNK):
        a = jnp.dot(xb, w1_ref[:, c0:c0 + FF_CHUNK], preferred_element_type=f32)
        b = jnp.dot(xb, w3_ref[:, c0:c0 + FF_CHUNK], preferred_element_type=f32)
        h = (jax.nn.silu(a) * b).astype(bf16)
        part = jnp.dot(h, w2_ref[c0:c0 + FF_CHUNK, :], preferred_element_type=f32)
        acc = part if acc is None else acc + part
    return acc


def _ffn_ln_kernel(alpha, x_ref, w1_ref, w3_ref, w2_ref, g_ref, b_ref, o_ref):
    for r0 in range(0, FFN_TILE, FFN_ROWS):
        x = x_ref[r0:r0 + FFN_ROWS, :]
        y = alpha * x + 0.5 * _swiglu(x.astype(bf16), w1_ref, w3_ref, w2_ref)
        o_ref[r0:r0 + FFN_ROWS, :] = _layernorm(y, g_ref[...], b_ref[...])


def _ffn_ln(x, w1, w3, w2, g, b, alpha):
    T = x.shape[0]
    row = pl.BlockSpec((FFN_TILE, D_MODEL), lambda i: (i, 0))
    return pl.pallas_call(
        functools.partial(_ffn_ln_kernel, alpha),
        grid=(T // FFN_TILE,),
        in_specs=[row, _const_spec(w1.shape), _const_spec(w3.shape), _const_spec(w2.shape),
                  _const_spec(g.shape), _const_spec(b.shape)],
        out_specs=row,
        out_shape=jax.ShapeDtypeStruct((T, D_MODEL), f32),
        compiler_params=pltpu.CompilerParams(dimension_semantics=("arbitrary",),
                                             vmem_limit_bytes=VMEM_LIMIT),
        name="ffn_ln",
    )(x, w1, w3, w2, g, b)


def _proj_kernel(h_ref, wp_ref, kvg_ref, wuk_ref, glng_ref, glnb_ref, ws_ref, bs_ref,
                 qabst_ref, c_ref, ct_ref, qit_ref, ki_ref, wit_ref, gm_ref):
    hb = h_ref[...].astype(bf16)
    proj = jnp.dot(hb, wp_ref[...], preferred_element_type=f32)

    for j in range(N_ATT_HEADS // 2):
        qj = proj[:, _PC_Q + j * LANES:_PC_Q + (j + 1) * LANES].astype(bf16)
        qa = jnp.dot(qj, wuk_ref[j], preferred_element_type=f32) * (ATT_HEAD_DIM ** -0.5 * LOG2E)
        for t in range(PROJ_TILE // Q_SUB):
            qabst_ref[t, 2 * j * KV_LATENT:2 * (j + 1) * KV_LATENT, :] = (
                qa[t * Q_SUB:(t + 1) * Q_SUB, :].T.astype(bf16))

    c = proj[:, _PC_C:_PC_C + KV_LATENT]
    c = c * lax.rsqrt(jnp.mean(c * c, axis=-1, keepdims=True) + LN_EPS) * kvg_ref[...]
    c_ref[...] = c.astype(bf16)
    ones_row = lax.broadcasted_iota(jnp.int32, (SUM_ROWS, PROJ_TILE), 0) == 0
    ct_ref[...] = jnp.concatenate([c.T, jnp.where(ones_row, 1.0, 0.0)], axis=0).astype(bf16)

    qi = proj[:, _PC_QI:_PC_QI + IDX_HEADS * IDX_DIM]
    kw = proj[:, _PC_KW:_PC_KW + LANES]
    ki_ref[...] = kw[:, :IDX_DIM].astype(bf16)
    for t in range(PROJ_TILE // Q_SUB):
        qit_ref[t] = qi[t * Q_SUB:(t + 1) * Q_SUB, :].T.astype(bf16)
        wit_ref[t] = kw[t * Q_SUB:(t + 1) * Q_SUB, :].T[IDX_DIM:IDX_DIM + SUBLANES, :]

    z = jax.nn.gelu(proj[:, _PC_Z:_PC_END])
    u = z[:, :MLP_WIDTH]
    v = _layernorm(z[:, MLP_WIDTH:], glng_ref[...], glnb_ref[...]).astype(bf16)
    r = lax.broadcasted_iota(jnp.int32, (CHUNK, CHUNK), 0)
    s = lax.broadcasted_iota(jnp.int32, (CHUNK, CHUNK), 1)
    tri = s <= r
    left = lax.broadcasted_iota(jnp.int32, (CHUNK, LANES), 1) < MLP_GROUP_DIM
    for t0 in range(0, PROJ_TILE, CHUNK):
        for j in range(N_MLP_GROUPS // 2):
            vj = v[t0:t0 + CHUNK, j * LANES:(j + 1) * LANES]
            wa = jnp.where(tri, ws_ref[2 * j], 0.0).astype(bf16)
            wb = jnp.where(tri, ws_ref[2 * j + 1], 0.0).astype(bf16)
            sa = jnp.dot(wa, vj, preferred_element_type=f32)
            sb = jnp.dot(wb, vj, preferred_element_type=f32)
            sv = jnp.where(left, sa, sb) + bs_ref[:, j * LANES:(j + 1) * LANES]
            gm_ref[t0:t0 + CHUNK, j * LANES:(j + 1) * LANES] = (
                u[t0:t0 + CHUNK, j * LANES:(j + 1) * LANES] * sv).astype(bf16)


def _proj(h, wp, kvg, wuk2, glng, glnb, ws, bs_full):
    T = h.shape[0]
    tm = PROJ_TILE

    def rows(n, dt):
        return pl.BlockSpec((tm, n), lambda i: (i, 0)), jax.ShapeDtypeStruct((T, n), dt)

    def cols(n, dt):
        return pl.BlockSpec((n, tm), lambda i: (0, i)), jax.ShapeDtypeStruct((n, T), dt)

    def qtiles(n, dt):
        return (pl.BlockSpec((tm // Q_SUB, n, Q_SUB), lambda i: (i, 0, 0)),
                jax.ShapeDtypeStruct((T // Q_SUB, n, Q_SUB), dt))

    outs = [qtiles(N_ATT_HEADS * KV_LATENT, bf16),
            rows(KV_LATENT, bf16),
            cols(KV_LATENT + SUM_ROWS, bf16),
            qtiles(IDX_HEADS * IDX_DIM, bf16),
            rows(IDX_DIM, bf16),
            qtiles(SUBLANES, f32),
            rows(MLP_WIDTH, bf16)]
    return pl.pallas_call(
        _proj_kernel,
        grid=(T // tm,),
        in_specs=[pl.BlockSpec((tm, D_MODEL), lambda i: (i, 0)),
                  _const_spec(wp.shape), _const_spec(kvg.shape), _const_spec(wuk2.shape),
                  _const_spec(glng.shape), _const_spec(glnb.shape), _const_spec(ws.shape),
                  _const_spec(bs_full.shape)],
        out_specs=[o[0] for o in outs],
        out_shape=[o[1] for o in outs],
        compiler_params=pltpu.CompilerParams(dimension_semantics=("arbitrary",),
                                             vmem_limit_bytes=VMEM_LIMIT),
        name="proj",
    )(h, wp, kvg, wuk2, glng, glnb, ws, bs_full)


def _attn_kernel(topk, qabst_ref, qit_ref, wit_ref, ki_ref, c_ref, ct_ref, wuvt_ref, o_ref,
                 st_ref, tau_ref, sca_ref, scb_ref, acc_ref, m_ref, mx_ref):
    i = pl.program_id(1)
    n_idx = i // (IDX_CHUNK // Q_TILE) + 1
    n_kv = i // (KV_CHUNK // Q_TILE) + 1
    qpos = i * Q_TILE + lax.broadcasted_iota(jnp.int32, (1, Q_TILE), 1)
    kf = float(topk)
    fold_shape = (FOLD_ROWS, Q_TILE)

    def fold_init(v):
        return jnp.full(fold_shape, v, f32)

    def fold_blocks(acc, s, fn):
        for t in range(0, s.shape[0], FOLD_ROWS):
            acc = fn(acc, s[t:t + FOLD_ROWS])
        return acc

    qcat = jnp.concatenate([qit_ref[u, h * IDX_DIM:(h + 1) * IDX_DIM, :]
                            for u in range(N_SUB) for h in range(IDX_HEADS)], axis=1)
    wit = [wit_ref[u] * (IDX_HEADS ** -0.5 * IDX_DIM ** -0.5) for u in range(N_SUB)]

    def scores(j):
        k0 = pl.multiple_of(j * IDX_CHUNK, IDX_CHUNK)
        lg = jnp.dot(ki_ref[pl.ds(k0, IDX_CHUNK), :], qcat, preferred_element_type=f32)
        parts = []
        for u in range(N_SUB):
            s = None
            for h in range(IDX_HEADS):
                col = (u * IDX_HEADS + h) * Q_SUB
                term = wit[u][h:h + 1, :] * jnp.maximum(lg[:, col:col + Q_SUB], 0.0)
                s = term if s is None else s + term
            parts.append(s)
        return k0, jnp.concatenate(parts, axis=1)

    def causal_chunk(j, carry):
        mn, mx = carry
        k0, s = scores(j)
        st_ref[pl.ds(k0, IDX_CHUNK), :] = s
        return fold_blocks(mn, s, jnp.minimum), fold_blocks(mx, s, jnp.maximum)

    def causal_pair(u, carry):
        return causal_chunk(2 * u + 1, causal_chunk(2 * u, carry))

    n_full = n_idx - 1
    mn, mx = lax.fori_loop(0, n_full // 2, causal_pair, (fold_init(jnp.inf), fold_init(-jnp.inf)))
    mn, mx = lax.cond(n_full % 2 == 1, lambda c: causal_chunk(n_full - 1, c), lambda c: c, (mn, mx))
    k0, s = scores(n_idx - 1)
    causal = k0 + lax.broadcasted_iota(jnp.int32, (IDX_CHUNK, 1), 0) <= qpos
    st_ref[pl.ds(k0, IDX_CHUNK), :] = jnp.where(causal, s, -jnp.inf)
    mn = fold_blocks(mn, jnp.where(causal, s, jnp.inf), jnp.minimum)
    mx = fold_blocks(mx, jnp.where(causal, s, -jnp.inf), jnp.maximum)

    def fold_keys(inits, fn):
        def body(j, accs):
            k0 = pl.multiple_of(j * IDX_CHUNK, IDX_CHUNK)
            for t in range(0, IDX_CHUNK, FOLD_ROWS):
                accs = fn(accs, st_ref[pl.ds(k0 + t, FOLD_ROWS), :])
            return accs
        return lax.fori_loop(0, n_idx, body, tuple(fold_init(v) for v in inits))

    def count_ge(v):
        vb = jnp.broadcast_to(v, fold_shape)
        acc, = fold_keys((0.0,), lambda a, s: (a[0] + jnp.where(s >= vb, 1.0, 0.0),))
        return jnp.sum(acc, axis=0, keepdims=True)

    @pl.when(i * Q_TILE + Q_TILE <= topk)
    def _():
        tau_ref[...] = jnp.full((1, Q_TILE), -jnp.inf, f32)

    @pl.when(i * Q_TILE + Q_TILE > topk)
    def _():
        lo = jnp.min(mn, axis=0, keepdims=True)
        top = jnp.max(mx, axis=0, keepdims=True)
        hi = top + jnp.maximum(jnp.abs(top) * 1e-3, 1e-30)

        def bisect(_, lh):
            lo, hi = lh
            mid = 0.5 * lo + 0.5 * hi
            ge = count_ge(mid) >= kf
            return jnp.where(ge, mid, lo), jnp.where(ge, hi, mid)

        def candidate(lo):
            lob = jnp.broadcast_to(lo, fold_shape)
            a, = fold_keys((jnp.inf,), lambda a, s: (
                jnp.minimum(a[0], jnp.where(s >= lob, s, jnp.inf)),))
            tau = jnp.min(a, axis=0, keepdims=True)
            taub = jnp.broadcast_to(tau, fold_shape)
            n_gt, n_eq = fold_keys((0.0, 0.0), lambda g, s: (
                g[0] + jnp.where(s > taub, 1.0, 0.0), g[1] + jnp.where(s == taub, 1.0, 0.0)))
            n_gt = jnp.sum(n_gt, axis=0, keepdims=True)
            n_eq = jnp.sum(n_eq, axis=0, keepdims=True)
            return tau, n_gt, n_eq, jnp.max(jnp.where(n_gt >= kf, 1.0, 0.0))

        lo, hi = lax.fori_loop(0, BISECT_STEPS, bisect, (lo, hi))

        def refine(state):
            _, hi, tau, n_gt, _, _ = state
            taub = jnp.broadcast_to(tau, fold_shape)
            nxt, = fold_keys((jnp.inf,), lambda a, s: (
                jnp.minimum(a[0], jnp.where(s > taub, s, jnp.inf)),))
            lo = jnp.where(n_gt >= kf, jnp.min(nxt, axis=0, keepdims=True), tau)
            lo, hi = lax.fori_loop(0, REFINE_STEPS, bisect, (lo, hi))
            return (lo, hi) + candidate(lo)

        _, _, tau, n_gt, n_eq, _ = lax.while_loop(lambda st: st[5] > 0.0, refine,
                                                  (lo, hi) + candidate(lo))
        tau_ref[...] = tau

        need = kf - n_gt
        surplus = n_eq > need

        @pl.when(jnp.max(jnp.where(surplus, 1.0, 0.0)) > 0.0)
        def _():
            ra = lax.broadcasted_iota(jnp.int32, (IDX_CHUNK, IDX_CHUNK), 0)
            rb = lax.broadcasted_iota(jnp.int32, (IDX_CHUNK, IDX_CHUNK), 1)
            earlier = jnp.where(rb < ra, 1.0, 0.0).astype(bf16)

            def drop(j, seen):
                k0 = pl.multiple_of(j * IDX_CHUNK, IDX_CHUNK)
                s = st_ref[pl.ds(k0, IDX_CHUNK), :]
                eq = jnp.where(s == tau, 1.0, 0.0)
                rank = jnp.dot(earlier, eq.astype(bf16), preferred_element_type=f32) + seen
                st_ref[pl.ds(k0, IDX_CHUNK), :] = jnp.where(
                    s == tau, jnp.where(rank >= need, -jnp.inf, s), s)
                return seen + jnp.sum(eq, axis=0, keepdims=True)

            lax.fori_loop(0, n_idx, drop, jnp.zeros((1, Q_TILE), f32))

    tau = tau_ref[...]
    eye = (lax.broadcasted_iota(jnp.int32, (Q_SUB, Q_SUB), 0)
           == lax.broadcasted_iota(jnp.int32, (Q_SUB, Q_SUB), 1))
    neg_eye = jnp.concatenate([jnp.where(eye, NEG_BIG, 0.0).astype(bf16)] * N_ATT_HEADS, axis=1)
    q_aug = [jnp.concatenate([qabst_ref[u, h * KV_LATENT:(h + 1) * KV_LATENT, :]
                              for h in range(N_ATT_HEADS)], axis=1) for u in range(N_SUB)]
    q_aug = [jnp.concatenate([q, neg_eye], axis=0) for q in q_aug]
    sub_cols = N_ATT_HEADS * Q_SUB

    def stage1(dst, j):
        k0 = pl.multiple_of(j * KV_CHUNK, KV_CHUNK)
        s = st_ref[pl.ds(k0, KV_CHUNK), :]
        kpos = k0 + lax.broadcasted_iota(jnp.int32, (KV_CHUNK, 1), 0)
        dropped = jnp.where(kpos <= qpos, jnp.where(s >= tau, 0.0, 1.0), 1.0).astype(bf16)
        cc = c_ref[pl.ds(k0, KV_CHUNK), :]
        for u in range(N_SUB):
            lhs = jnp.concatenate([cc, dropped[:, u * Q_SUB:(u + 1) * Q_SUB]], axis=1)
            sc = jnp.dot(lhs, q_aug[u], preferred_element_type=f32)
            dst[:, u * sub_cols:(u + 1) * sub_cols] = sc
            mx_ref[:, u * sub_cols:(u + 1) * sub_cols] = jnp.max(sc, axis=0, keepdims=True)

    def stage2(src, j, mx):
        k0 = pl.multiple_of(j * KV_CHUNK, KV_CHUNK)
        m_old = m_ref[...]
        m_new = jnp.maximum(m_old, mx)
        scale = jnp.exp2(m_old - m_new)
        p = jnp.exp2(src[...] - m_new)
        m_ref[...] = m_new
        pv = jnp.dot(ct_ref[:, pl.ds(k0, KV_CHUNK)], p.astype(bf16), preferred_element_type=f32)
        acc_ref[...] = scale * acc_ref[...] + pv

    def step(src, dst, t):
        mx = mx_ref[...]
        stage1(dst, t + 1)
        stage2(src, t, mx)

    m_ref[...] = jnp.full(m_ref.shape, NEG_BIG, f32)
    acc_ref[...] = jnp.zeros(acc_ref.shape, f32)
    stage1(sca_ref, 0)
    n_steps = n_kv - 1

    def two_steps(u, carry):
        step(sca_ref, scb_ref, 2 * u)
        step(scb_ref, sca_ref, 2 * u + 1)
        return carry

    lax.fori_loop(0, n_steps // 2, two_steps, 0)

    @pl.when(n_steps % 2 == 1)
    def _():
        step(sca_ref, scb_ref, n_steps - 1)
        stage2(scb_ref, n_steps, mx_ref[...])

    @pl.when(n_steps % 2 == 0)
    def _():
        stage2(sca_ref, n_steps, mx_ref[...])

    o_lat = acc_ref[:KV_LATENT, :] / acc_ref[KV_LATENT:KV_LATENT + 1, :]
    for u in range(N_SUB):
        for g in range(N_ATT_HEADS // 2):
            col = u * sub_cols + 2 * g * Q_SUB
            pair = jnp.concatenate([o_lat[:, col:col + Q_SUB], o_lat[:, col + Q_SUB:col + 2 * Q_SUB]],
                                   axis=0).astype(bf16)
            att_t = jnp.dot(wuvt_ref[g], pair, preferred_element_type=f32)
            o_ref[u * Q_SUB:(u + 1) * Q_SUB, g * LANES:(g + 1) * LANES] = att_t.T.astype(bf16)


def _attn(qabst, qit, wit, ki, c, ct, wuvt2, batch, seq, topk):
    T = batch * seq
    nq = seq // Q_TILE

    def qcol(n):
        return pl.BlockSpec((N_SUB, n, Q_SUB), lambda b, i: (b * nq + i, 0, 0))

    return pl.pallas_call(
        functools.partial(_attn_kernel, topk),
        grid=(batch, nq),
        in_specs=[qcol(N_ATT_HEADS * KV_LATENT), qcol(IDX_HEADS * IDX_DIM), qcol(SUBLANES),
                  pl.BlockSpec((seq, IDX_DIM), lambda b, i: (b, 0)),
                  pl.BlockSpec((seq, KV_LATENT), lambda b, i: (b, 0)),
                  pl.BlockSpec((KV_LATENT + SUM_ROWS, seq), lambda b, i: (0, b)),
                  pl.BlockSpec(wuvt2.shape, lambda b, i: (0, 0, 0), pipeline_mode=pl.Buffered(1))],
        out_specs=pl.BlockSpec((Q_TILE, ATT_WIDTH), lambda b, i: (b * nq + i, 0)),
        out_shape=jax.ShapeDtypeStruct((T, ATT_WIDTH), bf16),
        scratch_shapes=[pltpu.VMEM((seq, Q_TILE), f32),
                        pltpu.VMEM((1, Q_TILE), f32),
                        pltpu.VMEM((KV_CHUNK, N_COLS), f32),
                        pltpu.VMEM((KV_CHUNK, N_COLS), f32),
                        pltpu.VMEM((KV_LATENT + SUM_ROWS, N_COLS), f32),
                        pltpu.VMEM((1, N_COLS), f32),
                        pltpu.VMEM((1, N_COLS), f32)],
        compiler_params=pltpu.CompilerParams(dimension_semantics=("arbitrary", "arbitrary"),
                                             vmem_limit_bytes=VMEM_LIMIT),
        name="attn",
    )(qabst, qit, wit, ki, c, ct, wuvt2)


def _out_ffn_ln_kernel(alpha, h_ref, att_ref, gm_ref, wo_ref, g2_ref, b2_ref,
                       w1_ref, w3_ref, w2_ref, g3_ref, b3_ref, o_ref):
    for r0 in range(0, FFN_TILE, FFN_ROWS):
        rows = slice(r0, r0 + FFN_ROWS)
        mix = jnp.concatenate([att_ref[rows, :], gm_ref[rows, :]], axis=1)
        m = jnp.dot(mix, wo_ref[...], preferred_element_type=f32)
        x = _layernorm(alpha * h_ref[rows, :] + m, g2_ref[...], b2_ref[...])
        y = alpha * x + 0.5 * _swiglu(x.astype(bf16), w1_ref, w3_ref, w2_ref)
        o_ref[rows, :] = _layernorm(y, g3_ref[...], b3_ref[...])


def _out_ffn_ln(h, att, gm, wo, g2, b2, w1, w3, w2, g3, b3, alpha):
    T = h.shape[0]
    row = pl.BlockSpec((FFN_TILE, D_MODEL), lambda i: (i, 0))
    half = pl.BlockSpec((FFN_TILE, ATT_WIDTH), lambda i: (i, 0))
    consts = [wo, g2, b2, w1, w3, w2, g3, b3]
    return pl.pallas_call(
        functools.partial(_out_ffn_ln_kernel, alpha),
        grid=(T // FFN_TILE,),
        in_specs=[row, half, half] + [_const_spec(a.shape) for a in consts],
        out_specs=row,
        out_shape=jax.ShapeDtypeStruct((T, D_MODEL), f32),
        compiler_params=pltpu.CompilerParams(dimension_semantics=("arbitrary",),
                                             vmem_limit_bytes=VMEM_LIMIT),
        name="out_ffn_ln",
    )(h, att, gm, *consts)


def _pack_w_in(w_in):
    splits = np.cumsum([ATT_WIDTH, KV_LATENT, IDX_HEADS * IDX_DIM, IDX_DIM, IDX_HEADS]).tolist()
    q, c, qi, ki, wi, z = jnp.split(w_in, splits, axis=-1)
    pad = jnp.zeros((w_in.shape[0], LANES - IDX_DIM - IDX_HEADS), w_in.dtype)
    return jnp.concatenate([q, c, qi, ki, wi, pad, z], axis=-1).astype(bf16)


def _pair_blockdiag(w):
    H, r, c = w.shape
    z = jnp.zeros((H // 2, r, c), w.dtype)
    top = jnp.concatenate([w[0::2], z], axis=2)
    bot = jnp.concatenate([z, w[1::2]], axis=2)
    return jnp.concatenate([top, bot], axis=1)


def kernel(x, ffn1_w1, ffn1_w3, ffn1_w2, ln1_g, ln1_b, w_in, kv_norm_g, w_uk, w_uv, gmlp_ln_g, gmlp_ln_b, gmlp_ws, gmlp_bs, w_out, ln2_g, ln2_b, ffn2_w1, ffn2_w3, ffn2_w2, ln3_g, ln3_b):
    B, L, D = x.shape
    depth = ffn1_w1.shape[0]
    alpha = float((2 * depth) ** 0.25)
    topk = min(TOPK_MAX, L // 4)
    assert D == D_MODEL and L % IDX_CHUNK == 0 and (B * L) % FFN_TILE == 0
    assert topk % Q_TILE == 0 and IDX_CHUNK % KV_CHUNK == 0 and KV_CHUNK % Q_TILE == 0
    assert PROJ_TILE % CHUNK == 0 and IDX_CHUNK % FOLD_ROWS == 0

    def vec(a):
        return a.reshape(1, -1)

    h = x.reshape(B * L, D)
    for l in range(depth):
        h = _ffn_ln(h, ffn1_w1[l].astype(bf16), ffn1_w3[l].astype(bf16), ffn1_w2[l].astype(bf16),
                    vec(ln1_g[l]), vec(ln1_b[l]), alpha)
        wuk2 = _pair_blockdiag(jnp.swapaxes(w_uk[l], 1, 2)).astype(bf16)
        wuvt2 = _pair_blockdiag(jnp.swapaxes(w_uv[l], 1, 2)).astype(bf16)
        bs_full = jnp.repeat(gmlp_bs[l].T, MLP_GROUP_DIM, axis=1)
        qabst, c, ct, qit, ki, wit, gm = _proj(
            h, _pack_w_in(w_in[l]), vec(kv_norm_g[l]), wuk2, vec(gmlp_ln_g[l]), vec(gmlp_ln_b[l]),
            gmlp_ws[l], bs_full)
        att = _attn(qabst, qit, wit, ki, c, ct, wuvt2, B, L, topk)
        h = _out_ffn_ln(h, att, gm, w_out[l].astype(bf16), vec(ln2_g[l]), vec(ln2_b[l]),
                        ffn2_w1[l].astype(bf16), ffn2_w3[l].astype(bf16), ffn2_w2[l].astype(bf16),
                        vec(ln3_g[l]), vec(ln3_b[l]), alpha)
    return h.reshape(B, L, D)
```

```python
import functools

import jax
import jax.numpy as jnp
import numpy as np
from jax import lax
from jax.experimental import pallas as pl
from jax.experimental.pallas import tpu as pltpu

D_MODEL = 1024
N_ATT_HEADS = 8
ATT_HEAD_DIM = 64
ATT_WIDTH = N_ATT_HEADS * ATT_HEAD_DIM
KV_LATENT = 128
IDX_HEADS = 4
IDX_DIM = 64
TOPK_MAX = 256
N_MLP_GROUPS = 8
MLP_GROUP_DIM = 64
MLP_WIDTH = N_MLP_GROUPS * MLP_GROUP_DIM
CHUNK = 128
D_FF = 2816
LN_EPS = 1e-5

LANES = 128
SUBLANES = 8
Q_SUB = LANES
N_SUB = 4
Q_TILE = N_SUB * Q_SUB
N_COLS = N_ATT_HEADS * Q_TILE
IDX_CHUNK = 512
KV_CHUNK = 256
STEP_UNROLL = 2
FOLD_ROWS = 64
FFN_TILE = 512
FFN_ROWS = 256
FF_CHUNK = 2816
PROJ_TILE = 512
BISECT_STEPS = 22
REFINE_STEPS = 3
VMEM_LIMIT = 56 * 1024 * 1024
NEG_BIG = -(2.0 ** 100)
LOG2E = 1.4426950408889634
SUM_ROWS = 16

_PC_Q = 0
_PC_C = ATT_WIDTH
_PC_QI = _PC_C + KV_LATENT
_PC_KW = _PC_QI + IDX_HEADS * IDX_DIM
_PC_Z = _PC_KW + LANES
_PC_END = _PC_Z + 2 * MLP_WIDTH

bf16 = jnp.bfloat16
f32 = jnp.float32


def _const_spec(shape):
    nd = len(shape)
    return pl.BlockSpec(shape, lambda *_: (0,) * nd, pipeline_mode=pl.Buffered(1))


def _layernorm(y, g, b):
    mu = jnp.mean(y, axis=-1, keepdims=True)
    d = y - mu
    var = jnp.mean(d * d, axis=-1, keepdims=True)
    return d * lax.rsqrt(var + LN_EPS) * g + b


def _swiglu(xb, w1_ref, w3_ref, w2_ref):
    acc = None
    for c0 in range(0, D_FF, FF_CHUNK):
        a = jnp.dot(xb, w1_ref[:, c0:c0 + FF_CHUNK], preferred_element_type=f32)
        b = jnp.dot(xb, w3_ref[:, c0:c0 + FF_CHUNK], preferred_element_type=f32)
        h = (jax.nn.silu(a) * b).astype(bf16)
        part = jnp.dot(h, w2_ref[c0:c0 + FF_CHUNK, :], preferred_element_type=f32)
        acc = part if acc is None else acc + part
    return acc


def _ffn_ln_kernel(alpha, x_ref, w1_ref, w3_ref, w2_ref, g_ref, b_ref, o_ref):
    for r0 in range(0, FFN_TILE, FFN_ROWS):
        x = x_ref[r0:r0 + FFN_ROWS, :]
        y = alpha * x + 0.5 * _swiglu(x.astype(bf16), w1_ref, w3_ref, w2_ref)
        o_ref[r0:r0 + FFN_ROWS, :] = _layernorm(y, g_ref[...], b_ref[...])


def _ffn_ln(x, w1, w3, w2, g, b, alpha):
    T = x.shape[0]
    row = pl.BlockSpec((FFN_TILE, D_MODEL), lambda i: (i, 0))
    return pl.pallas_call(
        functools.partial(_ffn_ln_kernel, alpha),
        grid=(T // FFN_TILE,),
        in_specs=[row, _const_spec(w1.shape), _const_spec(w3.shape), _const_spec(w2.shape),
                  _const_spec(g.shape), _const_spec(b.shape)],
        out_specs=row,
        out_shape=jax.ShapeDtypeStruct((T, D_MODEL), f32),
        compiler_params=pltpu.CompilerParams(dimension_semantics=("arbitrary",),
                                             vmem_limit_bytes=VMEM_LIMIT),
        name="ffn_ln",
    )(x, w1, w3, w2, g, b)


def _proj_kernel(h_ref, wp_ref, kvg_ref, wuk_ref, glng_ref, glnb_ref, ws_ref, bs_ref,
                 qabst_ref, c_ref, ct_ref, qit_ref, ki_ref, wit_ref, gm_ref):
    hb = h_ref[...].astype(bf16)
    proj = jnp.dot(hb, wp_ref[...], preferred_element_type=f32)

    for j in range(N_ATT_HEADS // 2):
        qj = proj[:, _PC_Q + j * LANES:_PC_Q + (j + 1) * LANES].astype(bf16)
        qa = jnp.dot(qj, wuk_ref[j], preferred_element_type=f32) * (ATT_HEAD_DIM ** -0.5 * LOG2E)
        for t in range(PROJ_TILE // Q_SUB):
            qabst_ref[t, 2 * j * KV_LATENT:2 * (j + 1) * KV_LATENT, :] = (
                qa[t * Q_SUB:(t + 1) * Q_SUB, :].T.astype(bf16))

    c = proj[:, _PC_C:_PC_C + KV_LATENT]
    c = c * lax.rsqrt(jnp.mean(c * c, axis=-1, keepdims=True) + LN_EPS) * kvg_ref[...]
    c_ref[...] = c.astype(bf16)
    ones_row = lax.broadcasted_iota(jnp.int32, (SUM_ROWS, PROJ_TILE), 0) == 0
    ct_ref[...] = jnp.concatenate([c.T, jnp.where(ones_row, 1.0, 0.0)], axis=0).astype(bf16)

    qi = proj[:, _PC_QI:_PC_QI + IDX_HEADS * IDX_DIM]
    kw = proj[:, _PC_KW:_PC_KW + LANES]
    ki_ref[...] = kw[:, :IDX_DIM].astype(bf16)
    for t in range(PROJ_TILE // Q_SUB):
        qit_ref[t] = qi[t * Q_SUB:(t + 1) * Q_SUB, :].T.astype(bf16)
        wit_ref[t] = kw[t * Q_SUB:(t + 1) * Q_SUB, :].T[IDX_DIM:IDX_DIM + SUBLANES, :]

    z = jax.nn.gelu(proj[:, _PC_Z:_PC_END])
    u = z[:, :MLP_WIDTH]
    v = _layernorm(z[:, MLP_WIDTH:], glng_ref[...], glnb_ref[...]).astype(bf16)
    r = lax.broadcasted_iota(jnp.int32, (CHUNK, CHUNK), 0)
    s = lax.broadcasted_iota(jnp.int32, (CHUNK, CHUNK), 1)
    tri = s <= r
    left = lax.broadcasted_iota(jnp.int32, (CHUNK, LANES), 1) < MLP_GROUP_DIM
    for t0 in range(0, PROJ_TILE, CHUNK):
        for j in range(N_MLP_GROUPS // 2):
            vj = v[t0:t0 + CHUNK, j * LANES:(j + 1) * LANES]
            wa = jnp.where(tri, ws_ref[2 * j], 0.0).astype(bf16)
            wb = jnp.where(tri, ws_ref[2 * j + 1], 0.0).astype(bf16)
            sa = jnp.dot(wa, vj, preferred_element_type=f32)
            sb = jnp.dot(wb, vj, preferred_element_type=f32)
            sv = jnp.where(left, sa, sb) + bs_ref[:, j * LANES:(j + 1) * LANES]
            gm_ref[t0:t0 + CHUNK, j * LANES:(j + 1) * LANES] = (
                u[t0:t0 + CHUNK, j * LANES:(j + 1) * LANES] * sv).astype(bf16)


def _proj(h, wp, kvg, wuk2, glng, glnb, ws, bs_full):
    T = h.shape[0]
    tm = PROJ_TILE

    def rows(n, dt):
        return pl.BlockSpec((tm, n), lambda i: (i, 0)), jax.ShapeDtypeStruct((T, n), dt)

    def cols(n, dt):
        return pl.BlockSpec((n, tm), lambda i: (0, i)), jax.ShapeDtypeStruct((n, T), dt)

    def qtiles(n, dt):
        return (pl.BlockSpec((tm // Q_SUB, n, Q_SUB), lambda i: (i, 0, 0)),
                jax.ShapeDtypeStruct((T // Q_SUB, n, Q_SUB), dt))

    outs = [qtiles(N_ATT_HEADS * KV_LATENT, bf16),
            rows(KV_LATENT, bf16),
            cols(KV_LATENT + SUM_ROWS, bf16),
            qtiles(IDX_HEADS * IDX_DIM, bf16),
            rows(IDX_DIM, bf16),
            qtiles(SUBLANES, f32),
            rows(MLP_WIDTH, bf16)]
    return pl.pallas_call(
        _proj_kernel,
        grid=(T // tm,),
        in_specs=[pl.BlockSpec((tm, D_MODEL), lambda i: (i, 0)),
                  _const_spec(wp.shape), _const_spec(kvg.shape), _const_spec(wuk2.shape),
                  _const_spec(glng.shape), _const_spec(glnb.shape), _const_spec(ws.shape),
                  _const_spec(bs_full.shape)],
        out_specs=[o[0] for o in outs],
        out_shape=[o[1] for o in outs],
        compiler_params=pltpu.CompilerParams(dimension_semantics=("arbitrary",),
                                             vmem_limit_bytes=VMEM_LIMIT),
        name="proj",
    )(h, wp, kvg, wuk2, glng, glnb, ws, bs_full)


def _attn_kernel(topk, qabst_ref, qit_ref, wit_ref, ki_ref, c_ref, ct_ref, wuvt_ref, o_ref,
                 st_ref, tau_ref, sca_ref, scb_ref, acc_ref, m_ref, mx_ref):
    i = pl.program_id(1)
    n_keys = (i + 1) * Q_TILE
    n_idx = (n_keys + IDX_CHUNK - 1) // IDX_CHUNK
    n_kv = (n_keys + KV_CHUNK - 1) // KV_CHUNK
    qpos = i * Q_TILE + lax.broadcasted_iota(jnp.int32, (1, Q_TILE), 1)
    kf = float(topk)
    fold_shape = (FOLD_ROWS, Q_TILE)

    def fold_init(v):
        return jnp.full(fold_shape, v, f32)

    def fold_blocks(acc, s, fn):
        for t in range(0, s.shape[0], FOLD_ROWS):
            acc = fn(acc, s[t:t + FOLD_ROWS])
        return acc

    qcat = jnp.concatenate([qit_ref[u, h * IDX_DIM:(h + 1) * IDX_DIM, :]
                            for u in range(N_SUB) for h in range(IDX_HEADS)], axis=1)
    wit = [wit_ref[u] * (IDX_HEADS ** -0.5 * IDX_DIM ** -0.5) for u in range(N_SUB)]

    def scores(j):
        k0 = pl.multiple_of(j * IDX_CHUNK, IDX_CHUNK)
        lg = jnp.dot(ki_ref[pl.ds(k0, IDX_CHUNK), :], qcat, preferred_element_type=f32)
        parts = []
        for u in range(N_SUB):
            s = None
            for h in range(IDX_HEADS):
                col = (u * IDX_HEADS + h) * Q_SUB
                term = wit[u][h:h + 1, :] * jnp.maximum(lg[:, col:col + Q_SUB], 0.0)
                s = term if s is None else s + term
            parts.append(s)
        return k0, jnp.concatenate(parts, axis=1)

    def causal_chunk(j, carry):
        mn, mx = carry
        k0, s = scores(j)
        st_ref[pl.ds(k0, IDX_CHUNK), :] = s
        return fold_blocks(mn, s, jnp.minimum), fold_blocks(mx, s, jnp.maximum)

    def causal_pair(u, carry):
        return causal_chunk(2 * u + 1, causal_chunk(2 * u, carry))

    n_full = n_idx - 1
    mn, mx = lax.fori_loop(0, n_full // 2, causal_pair, (fold_init(jnp.inf), fold_init(-jnp.inf)))
    mn, mx = lax.cond(n_full % 2 == 1, lambda c: causal_chunk(n_full - 1, c), lambda c: c, (mn, mx))
    k0, s = scores(n_idx - 1)
    causal = k0 + lax.broadcasted_iota(jnp.int32, (IDX_CHUNK, 1), 0) <= qpos
    st_ref[pl.ds(k0, IDX_CHUNK), :] = jnp.where(causal, s, -jnp.inf)
    mn = fold_blocks(mn, jnp.where(causal, s, jnp.inf), jnp.minimum)
    mx = fold_blocks(mx, jnp.where(causal, s, -jnp.inf), jnp.maximum)

    def fold_keys(inits, fn):
        def body(j, accs):
            k0 = pl.multiple_of(j * IDX_CHUNK, IDX_CHUNK)
            for t in range(0, IDX_CHUNK, FOLD_ROWS):
                accs = fn(accs, st_ref[pl.ds(k0 + t, FOLD_ROWS), :])
            return accs
        return lax.fori_loop(0, n_idx, body, tuple(fold_init(v) for v in inits))

    def count_ge(v):
        vb = jnp.broadcast_to(v, fold_shape)
        acc, = fold_keys((0.0,), lambda a, s: (a[0] + jnp.where(s >= vb, 1.0, 0.0),))
        return jnp.sum(acc, axis=0, keepdims=True)

    @pl.when(i * Q_TILE + Q_TILE <= topk)
    def _():
        tau_ref[...] = jnp.full((1, Q_TILE), -jnp.inf, f32)

    @pl.when(i * Q_TILE + Q_TILE > topk)
    def _():
        lo = jnp.min(mn, axis=0, keepdims=True)
        top = jnp.max(mx, axis=0, keepdims=True)
        hi = top + jnp.maximum(jnp.abs(top) * 1e-3, 1e-30)
        n_lo = (qpos + 1).astype(f32)

        def bisect(_, state):
            lo, hi, n_lo = state
            mid = 0.5 * lo + 0.5 * hi
            n_mid = count_ge(mid)
            ge = n_mid >= kf
            return jnp.where(ge, mid, lo), jnp.where(ge, hi, mid), jnp.where(ge, n_mid, n_lo)

        def candidate(lo, n_lo):
            lob = jnp.broadcast_to(lo, fold_shape)
            a, = fold_keys((jnp.inf,), lambda a, s: (
                jnp.minimum(a[0], jnp.where(s >= lob, s, jnp.inf)),))
            tau = jnp.min(a, axis=0, keepdims=True)
            taub = jnp.broadcast_to(tau, fold_shape)
            n_gt, = fold_keys((0.0,), lambda g, s: (g[0] + jnp.where(s > taub, 1.0, 0.0),))
            n_gt = jnp.sum(n_gt, axis=0, keepdims=True)
            return tau, n_gt, n_lo - n_gt, jnp.max(jnp.where(n_gt >= kf, 1.0, 0.0))

        lo, hi, n_lo = lax.fori_loop(0, BISECT_STEPS, bisect, (lo, hi, n_lo))

        def refine(state):
            _, hi, n_lo, tau, n_gt, _, _ = state
            taub = jnp.broadcast_to(tau, fold_shape)
            nxt, = fold_keys((jnp.inf,), lambda a, s: (
                jnp.minimum(a[0], jnp.where(s > taub, s, jnp.inf)),))
            open_row = n_gt >= kf
            lo = jnp.where(open_row, jnp.min(nxt, axis=0, keepdims=True), tau)
            n_lo = jnp.where(open_row, n_gt, n_lo)
            lo, hi, n_lo = lax.fori_loop(0, REFINE_STEPS, bisect, (lo, hi, n_lo))
            return (lo, hi, n_lo) + candidate(lo, n_lo)

        _, _, _, tau, n_gt, n_eq, _ = lax.while_loop(lambda st: st[6] > 0.0, refine,
                                                     (lo, hi, n_lo) + candidate(lo, n_lo))
        tau_ref[...] = jnp.where(qpos < topk, -jnp.inf, tau)

        need = kf - n_gt
        surplus = n_eq > need

        @pl.when(jnp.max(jnp.where(surplus, 1.0, 0.0)) > 0.0)
        def _():
            ra = lax.broadcasted_iota(jnp.int32, (IDX_CHUNK, IDX_CHUNK), 0)
            rb = lax.broadcasted_iota(jnp.int32, (IDX_CHUNK, IDX_CHUNK), 1)
            earlier = jnp.where(rb < ra, 1.0, 0.0).astype(bf16)

            def drop(j, seen):
                k0 = pl.multiple_of(j * IDX_CHUNK, IDX_CHUNK)
                s = st_ref[pl.ds(k0, IDX_CHUNK), :]
                eq = jnp.where(s == tau, 1.0, 0.0)
                rank = jnp.dot(earlier, eq.astype(bf16), preferred_element_type=f32) + seen
                st_ref[pl.ds(k0, IDX_CHUNK), :] = jnp.where(
                    s == tau, jnp.where(rank >= need, -jnp.inf, s), s)
                return seen + jnp.sum(eq, axis=0, keepdims=True)

            lax.fori_loop(0, n_idx, drop, jnp.zeros((1, Q_TILE), f32))

    tau = tau_ref[...]
    eye = (lax.broadcasted_iota(jnp.int32, (Q_SUB, Q_SUB), 0)
           == lax.broadcasted_iota(jnp.int32, (Q_SUB, Q_SUB), 1))
    neg_eye = jnp.concatenate([jnp.where(eye, NEG_BIG, 0.0).astype(bf16)] * N_ATT_HEADS, axis=1)
    q_aug = [jnp.concatenate([qabst_ref[u, h * KV_LATENT:(h + 1) * KV_LATENT, :]
                              for h in range(N_ATT_HEADS)], axis=1) for u in range(N_SUB)]
    q_aug = [jnp.concatenate([q, neg_eye], axis=0) for q in q_aug]
    sub_cols = N_ATT_HEADS * Q_SUB

    def stage1(dst, j):
        k0 = pl.multiple_of(j * KV_CHUNK, KV_CHUNK)
        s = st_ref[pl.ds(k0, KV_CHUNK), :]
        kpos = k0 + lax.broadcasted_iota(jnp.int32, (KV_CHUNK, 1), 0)
        dropped = jnp.where(kpos <= qpos, jnp.where(s >= tau, 0.0, 1.0), 1.0).astype(bf16)
        cc = c_ref[pl.ds(k0, KV_CHUNK), :]
        for u in range(N_SUB):
            lhs = jnp.concatenate([cc, dropped[:, u * Q_SUB:(u + 1) * Q_SUB]], axis=1)
            sc = jnp.dot(lhs, q_aug[u], preferred_element_type=f32)
            dst[:, u * sub_cols:(u + 1) * sub_cols] = sc
            mx_ref[:, u * sub_cols:(u + 1) * sub_cols] = jnp.max(sc, axis=0, keepdims=True)

    def stage2(src, j, mx):
        k0 = pl.multiple_of(j * KV_CHUNK, KV_CHUNK)
        m_old = m_ref[...]
        m_new = jnp.maximum(m_old, mx)
        scale = jnp.exp2(m_old - m_new)
        p = jnp.exp2(src[...] - m_new)
        m_ref[...] = m_new
        pv = jnp.dot(ct_ref[:, pl.ds(k0, KV_CHUNK)], p.astype(bf16), preferred_element_type=f32)
        acc_ref[...] = scale * acc_ref[...] + pv

    def step(src, dst, t):
        mx = mx_ref[...]
        stage1(dst, t + 1)
        stage2(src, t, mx)

    m_ref[...] = jnp.full(m_ref.shape, NEG_BIG, f32)
    acc_ref[...] = jnp.zeros(acc_ref.shape, f32)
    stage1(sca_ref, 0)
    n_steps = n_kv - 1

    bufs = (sca_ref, scb_ref)

    def steps(t0, n):
        for r in range(n):
            step(bufs[r % 2], bufs[(r + 1) % 2], t0 + r)

    def unrolled(u, carry):
        steps(STEP_UNROLL * u, STEP_UNROLL)
        return carry

    lax.fori_loop(0, n_steps // STEP_UNROLL, unrolled, 0)
    for rem in range(STEP_UNROLL):
        @pl.when(n_steps % STEP_UNROLL == rem)
        def _(rem=rem):
            steps(n_steps - rem, rem)
            stage2(bufs[rem % 2], n_steps, mx_ref[...])

    o_lat = acc_ref[:KV_LATENT, :] / acc_ref[KV_LATENT:KV_LATENT + 1, :]
    for u in range(N_SUB):
        for g in range(N_ATT_HEADS // 2):
            col = u * sub_cols + 2 * g * Q_SUB
            pair = jnp.concatenate([o_lat[:, col:col + Q_SUB], o_lat[:, col + Q_SUB:col + 2 * Q_SUB]],
                                   axis=0).astype(bf16)
            att_t = jnp.dot(wuvt_ref[g], pair, preferred_element_type=f32)
            o_ref[u * Q_SUB:(u + 1) * Q_SUB, g * LANES:(g + 1) * LANES] = att_t.T.astype(bf16)


def _attn(qabst, qit, wit, ki, c, ct, wuvt2, batch, seq, topk):
    T = batch * seq
    nq = seq // Q_TILE

    def qcol(n):
        return pl.BlockSpec((N_SUB, n, Q_SUB), lambda b, i: (b * nq + i, 0, 0))

    return pl.pallas_call(
        functools.partial(_attn_kernel, topk),
        grid=(batch, nq),
        in_specs=[qcol(N_ATT_HEADS * KV_LATENT), qcol(IDX_HEADS * IDX_DIM), qcol(SUBLANES),
                  pl.BlockSpec((seq, IDX_DIM), lambda b, i: (b, 0)),
                  pl.BlockSpec((seq, KV_LATENT), lambda b, i: (b, 0)),
                  pl.BlockSpec((KV_LATENT + SUM_ROWS, seq), lambda b, i: (0, b)),
                  pl.BlockSpec(wuvt2.shape, lambda b, i: (0, 0, 0), pipeline_mode=pl.Buffered(1))],
        out_specs=pl.BlockSpec((Q_TILE, ATT_WIDTH), lambda b, i: (b * nq + i, 0)),
        out_shape=jax.ShapeDtypeStruct((T, ATT_WIDTH), bf16),
        scratch_shapes=[pltpu.VMEM((seq, Q_TILE), f32),
                        pltpu.VMEM((1, Q_TILE), f32),
                        pltpu.VMEM((KV_CHUNK, N_COLS), f32),
                        pltpu.VMEM((KV_CHUNK, N_COLS), f32),
                        pltpu.VMEM((KV_LATENT + SUM_ROWS, N_COLS), f32),
                        pltpu.VMEM((1, N_COLS), f32),
                        pltpu.VMEM((1, N_COLS), f32)],
        compiler_params=pltpu.CompilerParams(dimension_semantics=("arbitrary", "arbitrary"),
                                             vmem_limit_bytes=VMEM_LIMIT),
        name="attn",
    )(qabst, qit, wit, ki, c, ct, wuvt2)


def _out_ffn_ln_kernel(alpha, h_ref, att_ref, gm_ref, wo_ref, g2_ref, b2_ref,
                       w1_ref, w3_ref, w2_ref, g3_ref, b3_ref, o_ref):
    for r0 in range(0, FFN_TILE, FFN_ROWS):
        rows = slice(r0, r0 + FFN_ROWS)
        mix = jnp.concatenate([att_ref[rows, :], gm_ref[rows, :]], axis=1)
        m = jnp.dot(mix, wo_ref[...], preferred_element_type=f32)
        x = _layernorm(alpha * h_ref[rows, :] + m, g2_ref[...], b2_ref[...])
        y = alpha * x + 0.5 * _swiglu(x.astype(bf16), w1_ref, w3_ref, w2_ref)
        o_ref[rows, :] = _layernorm(y, g3_ref[...], b3_ref[...])


def _out_ffn_ln(h, att, gm, wo, g2, b2, w1, w3, w2, g3, b3, alpha):
    T = h.shape[0]
    row = pl.BlockSpec((FFN_TILE, D_MODEL), lambda i: (i, 0))
    half = pl.BlockSpec((FFN_TILE, ATT_WIDTH), lambda i: (i, 0))
    consts = [wo, g2, b2, w1, w3, w2, g3, b3]
    return pl.pallas_call(
        functools.partial(_out_ffn_ln_kernel, alpha),
        grid=(T // FFN_TILE,),
        in_specs=[row, half, half] + [_const_spec(a.shape) for a in consts],
        out_specs=row,
        out_shape=jax.ShapeDtypeStruct((T, D_MODEL), f32),
        compiler_params=pltpu.CompilerParams(dimension_semantics=("arbitrary",),
                                             vmem_limit_bytes=VMEM_LIMIT),
        name="out_ffn_ln",
    )(h, att, gm, *consts)


def _pack_w_in(w_in):
    splits = np.cumsum([ATT_WIDTH, KV_LATENT, IDX_HEADS * IDX_DIM, IDX_DIM, IDX_HEADS]).tolist()
    q, c, qi, ki, wi, z = jnp.split(w_in, splits, axis=-1)
    pad = jnp.zeros((w_in.shape[0], LANES - IDX_DIM - IDX_HEADS), w_in.dtype)
    return jnp.concatenate([q, c, qi, ki, wi, pad, z], axis=-1).astype(bf16)


def _pair_blockdiag(w):
    H, r, c = w.shape
    z = jnp.zeros((H // 2, r, c), w.dtype)
    top = jnp.concatenate([w[0::2], z], axis=2)
    bot = jnp.concatenate([z, w[1::2]], axis=2)
    return jnp.concatenate([top, bot], axis=1)


def kernel(x, ffn1_w1, ffn1_w3, ffn1_w2, ln1_g, ln1_b, w_in, kv_norm_g, w_uk, w_uv, gmlp_ln_g, gmlp_ln_b, gmlp_ws, gmlp_bs, w_out, ln2_g, ln2_b, ffn2_w1, ffn2_w3, ffn2_w2, ln3_g, ln3_b):
    B, L, D = x.shape
    depth = ffn1_w1.shape[0]
    alpha = float((2 * depth) ** 0.25)
    topk = min(TOPK_MAX, L // 4)
    assert D == D_MODEL and L % IDX_CHUNK == 0 and (B * L) % FFN_TILE == 0
    assert IDX_CHUNK % KV_CHUNK == 0 and IDX_CHUNK % Q_TILE == 0 and L % Q_TILE == 0
    assert max(Q_TILE, KV_CHUNK) % min(Q_TILE, KV_CHUNK) == 0
    assert PROJ_TILE % CHUNK == 0 and IDX_CHUNK % FOLD_ROWS == 0

    def vec(a):
        return a.reshape(1, -1)

    h = x.reshape(B * L, D)
    for l in range(depth):
        h = _ffn_ln(h, ffn1_w1[l].astype(bf16), ffn1_w3[l].astype(bf16), ffn1_w2[l].astype(bf16),
                    vec(ln1_g[l]), vec(ln1_b[l]), alpha)
        wuk2 = _pair_blockdiag(jnp.swapaxes(w_uk[l], 1, 2)).astype(bf16)
        wuvt2 = _pair_blockdiag(jnp.swapaxes(w_uv[l], 1, 2)).astype(bf16)
        bs_full = jnp.repeat(gmlp_bs[l].T, MLP_GROUP_DIM, axis=1)
        qabst, c, ct, qit, ki, wit, gm = _proj(
            h, _pack_w_in(w_in[l]), vec(kv_norm_g[l]), wuk2, vec(gmlp_ln_g[l]), vec(gmlp_ln_b[l]),
            gmlp_ws[l], bs_full)
        att = _attn(qabst, qit, wit, ki, c, ct, wuvt2, B, L, topk)
        h = _out_ffn_ln(h, att, gm, w_out[l].astype(bf16), vec(ln2_g[l]), vec(ln2_b[l]),
                        ffn2_w1[l].astype(bf16), ffn2_w3[l].astype(bf16), ffn2_w2[l].astype(bf16),
                        vec(ln3_g[l]), vec(ln3_b[l]), alpha)
    return h.reshape(B, L, D)
```

```python
import functools

import jax
import jax.numpy as jnp
import numpy as np
from jax import lax
from jax.experimental import pallas as pl
from jax.experimental.pallas import tpu as pltpu

D_MODEL = 1024
N_ATT_HEADS = 8
ATT_HEAD_DIM = 64
ATT_WIDTH = N_ATT_HEADS * ATT_HEAD_DIM
KV_LATENT = 128
IDX_HEADS = 4
IDX_DIM = 64
TOPK_MAX = 256
N_MLP_GROUPS = 8
MLP_GROUP_DIM = 64
MLP_WIDTH = N_MLP_GROUPS * MLP_GROUP_DIM
CHUNK = 128
D_FF = 2816
LN_EPS = 1e-5

LANES = 128
SUBLANES = 8
Q_SUB = LANES
N_SUB = 4
Q_TILE = N_SUB * Q_SUB
N_COLS = N_ATT_HEADS * Q_TILE
IDX_CHUNK = 512
KV_CHUNK = 512
STEP_UNROLL = 2
FOLD_ROWS = 64
FFN_TILE = 512
FFN_ROWS = 256
FF_CHUNK = 2816
PROJ_TILE = 512
BISECT_STEPS = 22
REFINE_STEPS = 1
VMEM_LIMIT = 56 * 1024 * 1024
NEG_BIG = -(2.0 ** 100)
LOG2E = 1.4426950408889634
SUM_ROWS = 16

_PC_Q = 0
_PC_C = ATT_WIDTH
_PC_QI = _PC_C + KV_LATENT
_PC_KW = _PC_QI + IDX_HEADS * IDX_DIM
_PC_Z = _PC_KW + LANES
_PC_END = _PC_Z + 2 * MLP_WIDTH

bf16 = jnp.bfloat16
f32 = jnp.float32


def _const_spec(shape):
    nd = len(shape)
    return pl.BlockSpec(shape, lambda *_: (0,) * nd, pipeline_mode=pl.Buffered(1))


def _layernorm(y, g, b):
    mu = jnp.mean(y, axis=-1, keepdims=True)
    d = y - mu
    var = jnp.mean(d * d, axis=-1, keepdims=True)
    return d * lax.rsqrt(var + LN_EPS) * g + b


def _swiglu(xb, w1_ref, w3_ref, w2_ref):
    acc = None
    for c0 in range(0, D_FF, FF_CHUNK):
        a = jnp.dot(xb, w1_ref[:, c0:c0 + FF_CHUNK], preferred_element_type=f32)
        b = jnp.dot(xb, w3_ref[:, c0:c0 + FF_CHUNK], preferred_element_type=f32)
        h = (jax.nn.silu(a) * b).astype(bf16)
        part = jnp.dot(h, w2_ref[c0:c0 + FF_CHUNK, :], preferred_element_type=f32)
        acc = part if acc is None else acc + part
    return acc


def _ffn_ln_kernel(alpha, x_ref, w1_ref, w3_ref, w2_ref, g_ref, b_ref, o_ref):
    for r0 in range(0, FFN_TILE, FFN_ROWS):
        x = x_ref[r0:r0 + FFN_ROWS, :]
        y = alpha * x + 0.5 * _swiglu(x.astype(bf16), w1_ref, w3_ref, w2_ref)
        o_ref[r0:r0 + FFN_ROWS, :] = _layernorm(y, g_ref[...], b_ref[...])


def _ffn_ln(x, w1, w3, w2, g, b, alpha):
    T = x.shape[0]
    row = pl.BlockSpec((FFN_TILE, D_MODEL), lambda i: (i, 0))
    return pl.pallas_call(
        functools.partial(_ffn_ln_kernel, alpha),
        grid=(T // FFN_TILE,),
        in_specs=[row, _const_spec(w1.shape), _const_spec(w3.shape), _const_spec(w2.shape),
                  _const_spec(g.shape), _const_spec(b.shape)],
        out_specs=row,
        out_shape=jax.ShapeDtypeStruct((T, D_MODEL), f32),
        compiler_params=pltpu.CompilerParams(dimension_semantics=("arbitrary",),
                                             vmem_limit_bytes=VMEM_LIMIT),
        name="ffn_ln",
    )(x, w1, w3, w2, g, b)


def _proj_kernel(h_ref, wp_ref, kvg_ref, wuk_ref, glng_ref, glnb_ref, ws_ref, bs_ref,
                 qabst_ref, c_ref, ct_ref, qit_ref, ki_ref, wit_ref, gm_ref):
    hb = h_ref[...].astype(bf16)
    proj = jnp.dot(hb, wp_ref[...], preferred_element_type=f32)

    for j in range(N_ATT_HEADS // 2):
        qj = proj[:, _PC_Q + j * LANES:_PC_Q + (j + 1) * LANES].astype(bf16)
        qa = jnp.dot(qj, wuk_ref[j], preferred_element_type=f32) * (ATT_HEAD_DIM ** -0.5 * LOG2E)
        for t in range(PROJ_TILE // Q_SUB):
            qabst_ref[t, 2 * j * KV_LATENT:2 * (j + 1) * KV_LATENT, :] = (
                qa[t * Q_SUB:(t + 1) * Q_SUB, :].T.astype(bf16))

    c = proj[:, _PC_C:_PC_C + KV_LATENT]
    c = c * lax.rsqrt(jnp.mean(c * c, axis=-1, keepdims=True) + LN_EPS) * kvg_ref[...]
    c_ref[...] = c.astype(bf16)
    ones_row = lax.broadcasted_iota(jnp.int32, (SUM_ROWS, PROJ_TILE), 0) == 0
    ct_ref[...] = jnp.concatenate([c.T, jnp.where(ones_row, 1.0, 0.0)], axis=0).astype(bf16)

    qi = proj[:, _PC_QI:_PC_QI + IDX_HEADS * IDX_DIM]
    kw = proj[:, _PC_KW:_PC_KW + LANES]
    ki_ref[...] = kw[:, :IDX_DIM].astype(bf16)
    for t in range(PROJ_TILE // Q_SUB):
        qit_ref[t] = qi[t * Q_SUB:(t + 1) * Q_SUB, :].T.astype(bf16)
        wit_ref[t] = kw[t * Q_SUB:(t + 1) * Q_SUB, :].T[IDX_DIM:IDX_DIM + SUBLANES, :]

    z = jax.nn.gelu(proj[:, _PC_Z:_PC_END])
    u = z[:, :MLP_WIDTH]
    v = _layernorm(z[:, MLP_WIDTH:], glng_ref[...], glnb_ref[...]).astype(bf16)
    r = lax.broadcasted_iota(jnp.int32, (CHUNK, CHUNK), 0)
    s = lax.broadcasted_iota(jnp.int32, (CHUNK, CHUNK), 1)
    tri = s <= r
    left = lax.broadcasted_iota(jnp.int32, (CHUNK, LANES), 1) < MLP_GROUP_DIM
    for t0 in range(0, PROJ_TILE, CHUNK):
        for j in range(N_MLP_GROUPS // 2):
            vj = v[t0:t0 + CHUNK, j * LANES:(j + 1) * LANES]
            wa = jnp.where(tri, ws_ref[2 * j], 0.0).astype(bf16)
            wb = jnp.where(tri, ws_ref[2 * j + 1], 0.0).astype(bf16)
            sa = jnp.dot(wa, vj, preferred_element_type=f32)
            sb = jnp.dot(wb, vj, preferred_element_type=f32)
            sv = jnp.where(left, sa, sb) + bs_ref[:, j * LANES:(j + 1) * LANES]
            gm_ref[t0:t0 + CHUNK, j * LANES:(j + 1) * LANES] = (
                u[t0:t0 + CHUNK, j * LANES:(j + 1) * LANES] * sv).astype(bf16)


def _proj(h, wp, kvg, wuk2, glng, glnb, ws, bs_full):
    T = h.shape[0]
    tm = PROJ_TILE

    def rows(n, dt):
        return pl.BlockSpec((tm, n), lambda i: (i, 0)), jax.ShapeDtypeStruct((T, n), dt)

    def cols(n, dt):
        return pl.BlockSpec((n, tm), lambda i: (0, i)), jax.ShapeDtypeStruct((n, T), dt)

    def qtiles(n, dt):
        return (pl.BlockSpec((tm // Q_SUB, n, Q_SUB), lambda i: (i, 0, 0)),
                jax.ShapeDtypeStruct((T // Q_SUB, n, Q_SUB), dt))

    outs = [qtiles(N_ATT_HEADS * KV_LATENT, bf16),
            rows(KV_LATENT, bf16),
            cols(KV_LATENT + SUM_ROWS, bf16),
            qtiles(IDX_HEADS * IDX_DIM, bf16),
            rows(IDX_DIM, bf16),
            qtiles(SUBLANES, f32),
            rows(MLP_WIDTH, bf16)]
    return pl.pallas_call(
        _proj_kernel,
        grid=(T // tm,),
        in_specs=[pl.BlockSpec((tm, D_MODEL), lambda i: (i, 0)),
                  _const_spec(wp.shape), _const_spec(kvg.shape), _const_spec(wuk2.shape),
                  _const_spec(glng.shape), _const_spec(glnb.shape), _const_spec(ws.shape),
                  _const_spec(bs_full.shape)],
        out_specs=[o[0] for o in outs],
        out_shape=[o[1] for o in outs],
        compiler_params=pltpu.CompilerParams(dimension_semantics=("arbitrary",),
                                             vmem_limit_bytes=VMEM_LIMIT),
        name="proj",
    )(h, wp, kvg, wuk2, glng, glnb, ws, bs_full)


def _attn_kernel(topk, qabst_ref, qit_ref, wit_ref, ki_ref, c_ref, ct_ref, wuvt_ref, o_ref,
                 st_ref, tau_ref, sca_ref, scb_ref, acc_ref, m_ref, mx_ref):
    i = pl.program_id(1)
    n_keys = (i + 1) * Q_TILE
    n_idx = (n_keys + IDX_CHUNK - 1) // IDX_CHUNK
    n_kv = (n_keys + KV_CHUNK - 1) // KV_CHUNK
    qpos = i * Q_TILE + lax.broadcasted_iota(jnp.int32, (1, Q_TILE), 1)
    kf = float(topk)
    fold_shape = (FOLD_ROWS, Q_TILE)

    def fold_init(v):
        return jnp.full(fold_shape, v, f32)

    def fold_blocks(acc, s, fn):
        for t in range(0, s.shape[0], FOLD_ROWS):
            acc = fn(acc, s[t:t + FOLD_ROWS])
        return acc

    qcat = jnp.concatenate([qit_ref[u, h * IDX_DIM:(h + 1) * IDX_DIM, :]
                            for u in range(N_SUB) for h in range(IDX_HEADS)], axis=1)
    wit = [wit_ref[u] * (IDX_HEADS ** -0.5 * IDX_DIM ** -0.5) for u in range(N_SUB)]

    def scores(j):
        k0 = pl.multiple_of(j * IDX_CHUNK, IDX_CHUNK)
        lg = jnp.dot(ki_ref[pl.ds(k0, IDX_CHUNK), :], qcat, preferred_element_type=f32)
        parts = []
        for u in range(N_SUB):
            s = None
            for h in range(IDX_HEADS):
                col = (u * IDX_HEADS + h) * Q_SUB
                term = wit[u][h:h + 1, :] * jnp.maximum(lg[:, col:col + Q_SUB], 0.0)
                s = term if s is None else s + term
            parts.append(s)
        return k0, jnp.concatenate(parts, axis=1)

    def causal_chunk(j, carry):
        mn, mx = carry
        k0, s = scores(j)
        st_ref[pl.ds(k0, IDX_CHUNK), :] = s
        return fold_blocks(mn, s, jnp.minimum), fold_blocks(mx, s, jnp.maximum)

    def causal_pair(u, carry):
        return causal_chunk(2 * u + 1, causal_chunk(2 * u, carry))

    n_full = n_idx - 1
    mn, mx = lax.fori_loop(0, n_full // 2, causal_pair, (fold_init(jnp.inf), fold_init(-jnp.inf)))
    mn, mx = lax.cond(n_full % 2 == 1, lambda c: causal_chunk(n_full - 1, c), lambda c: c, (mn, mx))
    k0, s = scores(n_idx - 1)
    causal = k0 + lax.broadcasted_iota(jnp.int32, (IDX_CHUNK, 1), 0) <= qpos
    st_ref[pl.ds(k0, IDX_CHUNK), :] = jnp.where(causal, s, -jnp.inf)
    mn = fold_blocks(mn, jnp.where(causal, s, jnp.inf), jnp.minimum)
    mx = fold_blocks(mx, jnp.where(causal, s, -jnp.inf), jnp.maximum)

    def fold_keys(inits, fn):
        def body(j, accs):
            k0 = pl.multiple_of(j * IDX_CHUNK, IDX_CHUNK)
            for t in range(0, IDX_CHUNK, FOLD_ROWS):
                accs = fn(accs, st_ref[pl.ds(k0 + t, FOLD_ROWS), :])
            return accs
        return lax.fori_loop(0, n_idx, body, tuple(fold_init(v) for v in inits))

    def count_ge(v):
        vb = jnp.broadcast_to(v, fold_shape)
        acc, = fold_keys((0.0,), lambda a, s: (a[0] + jnp.where(s >= vb, 1.0, 0.0),))
        return jnp.sum(acc, axis=0, keepdims=True)

    @pl.when(i * Q_TILE + Q_TILE <= topk)
    def _():
        tau_ref[...] = jnp.full((1, Q_TILE), -jnp.inf, f32)

    @pl.when(i * Q_TILE + Q_TILE > topk)
    def _():
        lo = jnp.min(mn, axis=0, keepdims=True)
        top = jnp.max(mx, axis=0, keepdims=True)
        hi = top + jnp.maximum(jnp.abs(top) * 1e-3, 1e-30)
        n_lo = (qpos + 1).astype(f32)

        def bisect(_, state):
            lo, hi, n_lo = state
            mid = 0.5 * lo + 0.5 * hi
            n_mid = count_ge(mid)
            ge = n_mid >= kf
            return jnp.where(ge, mid, lo), jnp.where(ge, hi, mid), jnp.where(ge, n_mid, n_lo)

        def candidate(lo, n_lo):
            lob = jnp.broadcast_to(lo, fold_shape)
            a, = fold_keys((jnp.inf,), lambda a, s: (
                jnp.minimum(a[0], jnp.where(s >= lob, s, jnp.inf)),))
            tau = jnp.min(a, axis=0, keepdims=True)
            taub = jnp.broadcast_to(tau, fold_shape)
            n_gt, = fold_keys((0.0,), lambda g, s: (g[0] + jnp.where(s > taub, 1.0, 0.0),))
            n_gt = jnp.sum(n_gt, axis=0, keepdims=True)
            return tau, n_gt, n_lo - n_gt, jnp.max(jnp.where(n_gt >= kf, 1.0, 0.0))

        lo, hi, n_lo = lax.fori_loop(0, BISECT_STEPS, bisect, (lo, hi, n_lo))

        def refine(state):
            _, hi, n_lo, tau, n_gt, _, _ = state
            taub = jnp.broadcast_to(tau, fold_shape)
            nxt, = fold_keys((jnp.inf,), lambda a, s: (
                jnp.minimum(a[0], jnp.where(s > taub, s, jnp.inf)),))
            open_row = n_gt >= kf
            lo = jnp.where(open_row, jnp.min(nxt, axis=0, keepdims=True), tau)
            n_lo = jnp.where(open_row, n_gt, n_lo)
            lo, hi, n_lo = lax.fori_loop(0, REFINE_STEPS, bisect, (lo, hi, n_lo))
            return (lo, hi, n_lo) + candidate(lo, n_lo)

        _, _, _, tau, n_gt, n_eq, _ = lax.while_loop(lambda st: st[6] > 0.0, refine,
                                                     (lo, hi, n_lo) + candidate(lo, n_lo))
        tau_ref[...] = jnp.where(qpos < topk, -jnp.inf, tau)

        need = kf - n_gt
        surplus = n_eq > need

        @pl.when(jnp.max(jnp.where(surplus, 1.0, 0.0)) > 0.0)
        def _():
            ra = lax.broadcasted_iota(jnp.int32, (IDX_CHUNK, IDX_CHUNK), 0)
            rb = lax.broadcasted_iota(jnp.int32, (IDX_CHUNK, IDX_CHUNK), 1)
            earlier = jnp.where(rb < ra, 1.0, 0.0).astype(bf16)

            def drop(j, seen):
                k0 = pl.multiple_of(j * IDX_CHUNK, IDX_CHUNK)
                s = st_ref[pl.ds(k0, IDX_CHUNK), :]
                eq = jnp.where(s == tau, 1.0, 0.0)
                rank = jnp.dot(earlier, eq.astype(bf16), preferred_element_type=f32) + seen
                st_ref[pl.ds(k0, IDX_CHUNK), :] = jnp.where(
                    s == tau, jnp.where(rank >= need, -jnp.inf, s), s)
                return seen + jnp.sum(eq, axis=0, keepdims=True)

            lax.fori_loop(0, n_idx, drop, jnp.zeros((1, Q_TILE), f32))

    tau = tau_ref[...]
    eye = (lax.broadcasted_iota(jnp.int32, (Q_SUB, Q_SUB), 0)
           == lax.broadcasted_iota(jnp.int32, (Q_SUB, Q_SUB), 1))
    neg_eye = jnp.concatenate([jnp.where(eye, NEG_BIG, 0.0).astype(bf16)] * N_ATT_HEADS, axis=1)
    q_aug = [jnp.concatenate([qabst_ref[u, h * KV_LATENT:(h + 1) * KV_LATENT, :]
                              for h in range(N_ATT_HEADS)], axis=1) for u in range(N_SUB)]
    q_aug = [jnp.concatenate([q, neg_eye], axis=0) for q in q_aug]
    sub_cols = N_ATT_HEADS * Q_SUB

    def stage1(dst, j):
        k0 = pl.multiple_of(j * KV_CHUNK, KV_CHUNK)
        s = st_ref[pl.ds(k0, KV_CHUNK), :]
        kpos = k0 + lax.broadcasted_iota(jnp.int32, (KV_CHUNK, 1), 0)
        dropped = jnp.where(kpos <= qpos, jnp.where(s >= tau, 0.0, 1.0), 1.0).astype(bf16)
        cc = c_ref[pl.ds(k0, KV_CHUNK), :]
        for u in range(N_SUB):
            lhs = jnp.concatenate([cc, dropped[:, u * Q_SUB:(u + 1) * Q_SUB]], axis=1)
            sc = jnp.dot(lhs, q_aug[u], preferred_element_type=f32)
            dst[:, u * sub_cols:(u + 1) * sub_cols] = sc
            mx_ref[:, u * sub_cols:(u + 1) * sub_cols] = jnp.max(sc, axis=0, keepdims=True)

    def stage2(src, j, mx):
        k0 = pl.multiple_of(j * KV_CHUNK, KV_CHUNK)
        m_old = m_ref[...]
        m_new = jnp.maximum(m_old, mx)
        scale = jnp.exp2(m_old - m_new)
        p = jnp.exp2(src[...] - m_new)
        m_ref[...] = m_new
        pv = jnp.dot(ct_ref[:, pl.ds(k0, KV_CHUNK)], p.astype(bf16), preferred_element_type=f32)
        acc_ref[...] = scale * acc_ref[...] + pv

    def step(src, dst, t):
        mx = mx_ref[...]
        stage1(dst, t + 1)
        stage2(src, t, mx)

    m_ref[...] = jnp.full(m_ref.shape, NEG_BIG, f32)
    acc_ref[...] = jnp.zeros(acc_ref.shape, f32)
    stage1(sca_ref, 0)
    n_steps = n_kv - 1

    bufs = (sca_ref, scb_ref)

    def steps(t0, n):
        for r in range(n):
            step(bufs[r % 2], bufs[(r + 1) % 2], t0 + r)

    def unrolled(u, carry):
        steps(STEP_UNROLL * u, STEP_UNROLL)
        return carry

    lax.fori_loop(0, n_steps // STEP_UNROLL, unrolled, 0)
    for rem in range(STEP_UNROLL):
        @pl.when(n_steps % STEP_UNROLL == rem)
        def _(rem=rem):
            steps(n_steps - rem, rem)
            stage2(bufs[rem % 2], n_steps, mx_ref[...])

    o_lat = acc_ref[:KV_LATENT, :] / acc_ref[KV_LATENT:KV_LATENT + 1, :]
    for u in range(N_SUB):
        for g in range(N_ATT_HEADS // 2):
            col = u * sub_cols + 2 * g * Q_SUB
            pair = jnp.concatenate([o_lat[:, col:col + Q_SUB], o_lat[:, col + Q_SUB:col + 2 * Q_SUB]],
                                   axis=0).astype(bf16)
            att_t = jnp.dot(wuvt_ref[g], pair, preferred_element_type=f32)
            o_ref[u * Q_SUB:(u + 1) * Q_SUB, g * LANES:(g + 1) * LANES] = att_t.T.astype(bf16)


def _attn(qabst, qit, wit, ki, c, ct, wuvt2, batch, seq, topk):
    T = batch * seq
    nq = seq // Q_TILE

    def qcol(n):
        return pl.BlockSpec((N_SUB, n, Q_SUB), lambda b, i: (b * nq + i, 0, 0))

    return pl.pallas_call(
        functools.partial(_attn_kernel, topk),
        grid=(batch, nq),
        in_specs=[qcol(N_ATT_HEADS * KV_LATENT), qcol(IDX_HEADS * IDX_DIM), qcol(SUBLANES),
                  pl.BlockSpec((seq, IDX_DIM), lambda b, i: (b, 0)),
                  pl.BlockSpec((seq, KV_LATENT), lambda b, i: (b, 0)),
                  pl.BlockSpec((KV_LATENT + SUM_ROWS, seq), lambda b, i: (0, b)),
                  pl.BlockSpec(wuvt2.shape, lambda b, i: (0, 0, 0), pipeline_mode=pl.Buffered(1))],
        out_specs=pl.BlockSpec((Q_TILE, ATT_WIDTH), lambda b, i: (b * nq + i, 0)),
        out_shape=jax.ShapeDtypeStruct((T, ATT_WIDTH), bf16),
        scratch_shapes=[pltpu.VMEM((seq, Q_TILE), f32),
                        pltpu.VMEM((1, Q_TILE), f32),
                        pltpu.VMEM((KV_CHUNK, N_COLS), f32),
                        pltpu.VMEM((KV_CHUNK, N_COLS), f32),
                        pltpu.VMEM((KV_LATENT + SUM_ROWS, N_COLS), f32),
                        pltpu.VMEM((1, N_COLS), f32),
                        pltpu.VMEM((1, N_COLS), f32)],
        compiler_params=pltpu.CompilerParams(dimension_semantics=("arbitrary", "arbitrary"),
                                             vmem_limit_bytes=VMEM_LIMIT),
        name="attn",
    )(qabst, qit, wit, ki, c, ct, wuvt2)


def _out_ffn_ln_kernel(alpha, h_ref, att_ref, gm_ref, wo_ref, g2_ref, b2_ref,
                       w1_ref, w3_ref, w2_ref, g3_ref, b3_ref, o_ref):
    for r0 in range(0, FFN_TILE, FFN_ROWS):
        rows = slice(r0, r0 + FFN_ROWS)
        mix = jnp.concatenate([att_ref[rows, :], gm_ref[rows, :]], axis=1)
        m = jnp.dot(mix, wo_ref[...], preferred_element_type=f32)
        x = _layernorm(alpha * h_ref[rows, :] + m, g2_ref[...], b2_ref[...])
        y = alpha * x + 0.5 * _swiglu(x.astype(bf16), w1_ref, w3_ref, w2_ref)
        o_ref[rows, :] = _layernorm(y, g3_ref[...], b3_ref[...])


def _out_ffn_ln(h, att, gm, wo, g2, b2, w1, w3, w2, g3, b3, alpha):
    T = h.shape[0]
    row = pl.BlockSpec((FFN_TILE, D_MODEL), lambda i: (i, 0))
    half = pl.BlockSpec((FFN_TILE, ATT_WIDTH), lambda i: (i, 0))
    consts = [wo, g2, b2, w1, w3, w2, g3, b3]
    return pl.pallas_call(
        functools.partial(_out_ffn_ln_kernel, alpha),
        grid=(T // FFN_TILE,),
        in_specs=[row, half, half] + [_const_spec(a.shape) for a in consts],
        out_specs=row,
        out_shape=jax.ShapeDtypeStruct((T, D_MODEL), f32),
        compiler_params=pltpu.CompilerParams(dimension_semantics=("arbitrary",),
                                             vmem_limit_bytes=VMEM_LIMIT),
        name="out_ffn_ln",
    )(h, att, gm, *consts)


def _pack_w_in(w_in):
    splits = np.cumsum([ATT_WIDTH, KV_LATENT, IDX_HEADS * IDX_DIM, IDX_DIM, IDX_HEADS]).tolist()
    q, c, qi, ki, wi, z = jnp.split(w_in, splits, axis=-1)
    pad = jnp.zeros((w_in.shape[0], LANES - IDX_DIM - IDX_HEADS), w_in.dtype)
    return jnp.concatenate([q, c, qi, ki, wi, pad, z], axis=-1).astype(bf16)


def _pair_blockdiag(w):
    H, r, c = w.shape
    z = jnp.zeros((H // 2, r, c), w.dtype)
    top = jnp.concatenate([w[0::2], z], axis=2)
    bot = jnp.concatenate([z, w[1::2]], axis=2)
    return jnp.concatenate([top, bot], axis=1)


def kernel(x, ffn1_w1, ffn1_w3, ffn1_w2, ln1_g, ln1_b, w_in, kv_norm_g, w_uk, w_uv, gmlp_ln_g, gmlp_ln_b, gmlp_ws, gmlp_bs, w_out, ln2_g, ln2_b, ffn2_w1, ffn2_w3, ffn2_w2, ln3_g, ln3_b):
    B, L, D = x.shape
    depth = ffn1_w1.shape[0]
    alpha = float((2 * depth) ** 0.25)
    topk = min(TOPK_MAX, L // 4)
    assert D == D_MODEL and L % IDX_CHUNK == 0 and (B * L) % FFN_TILE == 0
    assert IDX_CHUNK % KV_CHUNK == 0 and IDX_CHUNK % Q_TILE == 0 and L % Q_TILE == 0
    assert max(Q_TILE, KV_CHUNK) % min(Q_TILE, KV_CHUNK) == 0
    assert PROJ_TILE % CHUNK == 0 and IDX_CHUNK % FOLD_ROWS == 0

    def vec(a):
        return a.reshape(1, -1)

    h = x.reshape(B * L, D)
    for l in range(depth):
        h = _ffn_ln(h, ffn1_w1[l].astype(bf16), ffn1_w3[l].astype(bf16), ffn1_w2[l].astype(bf16),
                    vec(ln1_g[l]), vec(ln1_b[l]), alpha)
        wuk2 = _pair_blockdiag(jnp.swapaxes(w_uk[l], 1, 2)).astype(bf16)
        wuvt2 = _pair_blockdiag(jnp.swapaxes(w_uv[l], 1, 2)).astype(bf16)
        bs_full = jnp.repeat(gmlp_bs[l].T, MLP_GROUP_DIM, axis=1)
        qabst, c, ct, qit, ki, wit, gm = _proj(
            h, _pack_w_in(w_in[l]), vec(kv_norm_g[l]), wuk2, vec(gmlp_ln_g[l]), vec(gmlp_ln_b[l]),
            gmlp_ws[l], bs_full)
        att = _attn(qabst, qit, wit, ki, c, ct, wuvt2, B, L, topk)
        h = _out_ffn_ln(h, att, gm, w_out[l].astype(bf16), vec(ln2_g[l]), vec(ln2_b[l]),
                        ffn2_w1[l].astype(bf16), ffn2_w3[l].astype(bf16), ffn2_w2[l].astype(bf16),
                        vec(ln3_g[l]), vec(ln3_b[l]), alpha)
    return h.reshape(B, L, D)
```

```python
import functools

import jax
import jax.numpy as jnp
import numpy as np
from jax import lax
from jax.experimental import pallas as pl
from jax.experimental.pallas import tpu as pltpu

D_MODEL = 1024
N_ATT_HEADS = 8
ATT_HEAD_DIM = 64
ATT_WIDTH = N_ATT_HEADS * ATT_HEAD_DIM
KV_LATENT = 128
IDX_HEADS = 4
IDX_DIM = 64
TOPK_MAX = 256
N_MLP_GROUPS = 8
MLP_GROUP_DIM = 64
MLP_WIDTH = N_MLP_GROUPS * MLP_GROUP_DIM
CHUNK = 128
D_FF = 2816
LN_EPS = 1e-5

LANES = 128
SUBLANES = 8
Q_SUB = LANES
N_SUB = 4
Q_TILE = N_SUB * Q_SUB
N_COLS = N_ATT_HEADS * Q_TILE
IDX_CHUNK = 512
KV_CHUNK = 512
STEP_UNROLL = 2
FOLD_ROWS = 64
FFN_TILE = 512
FFN_ROWS = 256
FF_CHUNK = 2816
PROJ_TILE = 512
BISECT_STEPS = 22
REFINE_STEPS = 1
VMEM_LIMIT = 56 * 1024 * 1024
NEG_BIG = -(2.0 ** 100)
LOG2E = 1.4426950408889634
SUM_ROWS = 16

_PC_Q = 0
_PC_C = ATT_WIDTH
_PC_QI = _PC_C + KV_LATENT
_PC_KW = _PC_QI + IDX_HEADS * IDX_DIM
_PC_Z = _PC_KW + LANES
_PC_END = _PC_Z + 2 * MLP_WIDTH

bf16 = jnp.bfloat16
f32 = jnp.float32


def _const_spec(shape):
    nd = len(shape)
    return pl.BlockSpec(shape, lambda *_: (0,) * nd, pipeline_mode=pl.Buffered(1))


def _layernorm(y, g, b):
    mu = jnp.mean(y, axis=-1, keepdims=True)
    d = y - mu
    var = jnp.mean(d * d, axis=-1, keepdims=True)
    return d * lax.rsqrt(var + LN_EPS) * g + b


def _swiglu(xb, w1_ref, w3_ref, w2_ref):
    acc = None
    for c0 in range(0, D_FF, FF_CHUNK):
        a = jnp.dot(xb, w1_ref[:, c0:c0 + FF_CHUNK], preferred_element_type=f32)
        b = jnp.dot(xb, w3_ref[:, c0:c0 + FF_CHUNK], preferred_element_type=f32)
        h = (jax.nn.silu(a) * b).astype(bf16)
        part = jnp.dot(h, w2_ref[c0:c0 + FF_CHUNK, :], preferred_element_type=f32)
        acc = part if acc is None else acc + part
    return acc


def _ffn_ln_kernel(alpha, x_ref, w1_ref, w3_ref, w2_ref, g_ref, b_ref, o_ref):
    for r0 in range(0, FFN_TILE, FFN_ROWS):
        x = x_ref[r0:r0 + FFN_ROWS, :]
        y = alpha * x + 0.5 * _swiglu(x.astype(bf16), w1_ref, w3_ref, w2_ref)
        o_ref[r0:r0 + FFN_ROWS, :] = _layernorm(y, g_ref[...], b_ref[...])


def _ffn_ln(x, w1, w3, w2, g, b, alpha):
    T = x.shape[0]
    row = pl.BlockSpec((FFN_TILE, D_MODEL), lambda i: (i, 0))
    return pl.pallas_call(
        functools.partial(_ffn_ln_kernel, alpha),
        grid=(T // FFN_TILE,),
        in_specs=[row, _const_spec(w1.shape), _const_spec(w3.shape), _const_spec(w2.shape),
                  _const_spec(g.shape), _const_spec(b.shape)],
        out_specs=row,
        out_shape=jax.ShapeDtypeStruct((T, D_MODEL), f32),
        compiler_params=pltpu.CompilerParams(dimension_semantics=("arbitrary",),
                                             vmem_limit_bytes=VMEM_LIMIT),
        name="ffn_ln",
    )(x, w1, w3, w2, g, b)


def _proj_kernel(h_ref, wp_ref, kvg_ref, wuk_ref, glng_ref, glnb_ref, ws_ref, bs_ref,
                 qabst_ref, c_ref, ct_ref, qit_ref, ki_ref, wit_ref, gm_ref):
    hb = h_ref[...].astype(bf16)
    proj = jnp.dot(hb, wp_ref[...], preferred_element_type=f32)

    for j in range(N_ATT_HEADS // 2):
        qj = proj[:, _PC_Q + j * LANES:_PC_Q + (j + 1) * LANES].astype(bf16)
        qa = jnp.dot(qj, wuk_ref[j], preferred_element_type=f32) * (ATT_HEAD_DIM ** -0.5 * LOG2E)
        for t in range(PROJ_TILE // Q_SUB):
            qabst_ref[t, 2 * j * KV_LATENT:2 * (j + 1) * KV_LATENT, :] = (
                qa[t * Q_SUB:(t + 1) * Q_SUB, :].T.astype(bf16))

    c = proj[:, _PC_C:_PC_C + KV_LATENT]
    c = c * lax.rsqrt(jnp.mean(c * c, axis=-1, keepdims=True) + LN_EPS) * kvg_ref[...]
    c_ref[...] = c.astype(bf16)
    ones_row = lax.broadcasted_iota(jnp.int32, (SUM_ROWS, PROJ_TILE), 0) == 0
    ct_ref[...] = jnp.concatenate([c.T, jnp.where(ones_row, 1.0, 0.0)], axis=0).astype(bf16)

    qi = proj[:, _PC_QI:_PC_QI + IDX_HEADS * IDX_DIM]
    kw = proj[:, _PC_KW:_PC_KW + LANES]
    ki_ref[...] = kw[:, :IDX_DIM].astype(bf16)
    for t in range(PROJ_TILE // Q_SUB):
        qit_ref[t] = qi[t * Q_SUB:(t + 1) * Q_SUB, :].T.astype(bf16)
        wit_ref[t] = kw[t * Q_SUB:(t + 1) * Q_SUB, :].T[IDX_DIM:IDX_DIM + SUBLANES, :]

    z = jax.nn.gelu(proj[:, _PC_Z:_PC_END])
    u = z[:, :MLP_WIDTH]
    v = _layernorm(z[:, MLP_WIDTH:], glng_ref[...], glnb_ref[...]).astype(bf16)
    r = lax.broadcasted_iota(jnp.int32, (CHUNK, CHUNK), 0)
    s = lax.broadcasted_iota(jnp.int32, (CHUNK, CHUNK), 1)
    tri = s <= r
    left = lax.broadcasted_iota(jnp.int32, (CHUNK, LANES), 1) < MLP_GROUP_DIM
    for t0 in range(0, PROJ_TILE, CHUNK):
        for j in range(N_MLP_GROUPS // 2):
            vj = v[t0:t0 + CHUNK, j * LANES:(j + 1) * LANES]
            wa = jnp.where(tri, ws_ref[2 * j], 0.0).astype(bf16)
            wb = jnp.where(tri, ws_ref[2 * j + 1], 0.0).astype(bf16)
            sa = jnp.dot(wa, vj, preferred_element_type=f32)
            sb = jnp.dot(wb, vj, preferred_element_type=f32)
            sv = jnp.where(left, sa, sb) + bs_ref[:, j * LANES:(j + 1) * LANES]
            gm_ref[t0:t0 + CHUNK, j * LANES:(j + 1) * LANES] = (
                u[t0:t0 + CHUNK, j * LANES:(j + 1) * LANES] * sv).astype(bf16)


def _proj(h, wp, kvg, wuk2, glng, glnb, ws, bs_full):
    T = h.shape[0]
    tm = PROJ_TILE

    def rows(n, dt):
        return pl.BlockSpec((tm, n), lambda i: (i, 0)), jax.ShapeDtypeStruct((T, n), dt)

    def cols(n, dt):
        return pl.BlockSpec((n, tm), lambda i: (0, i)), jax.ShapeDtypeStruct((n, T), dt)

    def qtiles(n, dt):
        return (pl.BlockSpec((tm // Q_SUB, n, Q_SUB), lambda i: (i, 0, 0)),
                jax.ShapeDtypeStruct((T // Q_SUB, n, Q_SUB), dt))

    outs = [qtiles(N_ATT_HEADS * KV_LATENT, bf16),
            rows(KV_LATENT, bf16),
            cols(KV_LATENT + SUM_ROWS, bf16),
            qtiles(IDX_HEADS * IDX_DIM, bf16),
            rows(IDX_DIM, bf16),
            qtiles(SUBLANES, f32),
            rows(MLP_WIDTH, bf16)]
    return pl.pallas_call(
        _proj_kernel,
        grid=(T // tm,),
        in_specs=[pl.BlockSpec((tm, D_MODEL), lambda i: (i, 0)),
                  _const_spec(wp.shape), _const_spec(kvg.shape), _const_spec(wuk2.shape),
                  _const_spec(glng.shape), _const_spec(glnb.shape), _const_spec(ws.shape),
                  _const_spec(bs_full.shape)],
        out_specs=[o[0] for o in outs],
        out_shape=[o[1] for o in outs],
        compiler_params=pltpu.CompilerParams(dimension_semantics=("arbitrary",),
                                             vmem_limit_bytes=VMEM_LIMIT),
        name="proj",
    )(h, wp, kvg, wuk2, glng, glnb, ws, bs_full)


def _attn_kernel(topk, qabst_ref, qit_ref, wit_ref, ki_ref, c_ref, ct_ref, wuvt_ref, o_ref,
                 st_ref, tau_ref, sca_ref, scb_ref, acc_ref, m_ref, mx_ref):
    i = pl.program_id(1)
    n_keys = (i + 1) * Q_TILE
    n_idx = (n_keys + IDX_CHUNK - 1) // IDX_CHUNK
    n_kv = (n_keys + KV_CHUNK - 1) // KV_CHUNK
    qpos = i * Q_TILE + lax.broadcasted_iota(jnp.int32, (1, Q_TILE), 1)
    kf = float(topk)
    fold_shape = (FOLD_ROWS, Q_TILE)

    def fold_init(v):
        return jnp.full(fold_shape, v, f32)

    def fold_blocks(acc, s, fn):
        for t in range(0, s.shape[0], FOLD_ROWS):
            acc = fn(acc, s[t:t + FOLD_ROWS])
        return acc

    qcat = jnp.concatenate([qit_ref[u, h * IDX_DIM:(h + 1) * IDX_DIM, :]
                            for u in range(N_SUB) for h in range(IDX_HEADS)], axis=1)
    wit = [wit_ref[u] * (IDX_HEADS ** -0.5 * IDX_DIM ** -0.5) for u in range(N_SUB)]

    def scores(j):
        k0 = pl.multiple_of(j * IDX_CHUNK, IDX_CHUNK)
        lg = jnp.dot(ki_ref[pl.ds(k0, IDX_CHUNK), :], qcat, preferred_element_type=f32)
        parts = []
        for u in range(N_SUB):
            s = None
            for h in range(IDX_HEADS):
                col = (u * IDX_HEADS + h) * Q_SUB
                term = wit[u][h:h + 1, :] * jnp.maximum(lg[:, col:col + Q_SUB], 0.0)
                s = term if s is None else s + term
            parts.append(s)
        return k0, jnp.concatenate(parts, axis=1)

    def causal_chunk(j, carry):
        mn, mx = carry
        k0, s = scores(j)
        st_ref[pl.ds(k0, IDX_CHUNK), :] = s
        return fold_blocks(mn, s, jnp.minimum), fold_blocks(mx, s, jnp.maximum)

    def causal_pair(u, carry):
        return causal_chunk(2 * u + 1, causal_chunk(2 * u, carry))

    n_full = n_idx - 1
    mn, mx = lax.fori_loop(0, n_full // 2, causal_pair, (fold_init(jnp.inf), fold_init(-jnp.inf)))
    mn, mx = lax.cond(n_full % 2 == 1, lambda c: causal_chunk(n_full - 1, c), lambda c: c, (mn, mx))
    k0, s = scores(n_idx - 1)
    causal = k0 + lax.broadcasted_iota(jnp.int32, (IDX_CHUNK, 1), 0) <= qpos
    st_ref[pl.ds(k0, IDX_CHUNK), :] = jnp.where(causal, s, -jnp.inf)
    mn = fold_blocks(mn, jnp.where(causal, s, jnp.inf), jnp.minimum)
    mx = fold_blocks(mx, jnp.where(causal, s, -jnp.inf), jnp.maximum)

    def fold_keys(inits, fn, rows):
        full = tuple(jnp.broadcast_to(r, fold_shape) for r in rows)

        def body(j, accs):
            k0 = pl.multiple_of(j * IDX_CHUNK, IDX_CHUNK)
            for t in range(0, IDX_CHUNK, FOLD_ROWS):
                accs = fn(accs, st_ref[pl.ds(k0 + t, FOLD_ROWS), :], *full)
            return accs

        accs = lax.fori_loop(0, n_idx - 1, body, tuple(fold_init(v) for v in inits))
        k0 = pl.multiple_of((n_idx - 1) * IDX_CHUNK, IDX_CHUNK)
        parts = []
        for u in range(N_SUB):
            cols = slice(u * Q_SUB, (u + 1) * Q_SUB)
            part = tuple(a[:, cols] for a in accs)
            sub = tuple(f[:, cols] for f in full)
            for t in range(0, (u + 1) * Q_SUB, FOLD_ROWS):
                part = fn(part, st_ref[pl.ds(k0 + t, FOLD_ROWS), cols], *sub)
            parts.append(part)
        return tuple(jnp.concatenate([p[a] for p in parts], axis=1) for a in range(len(inits)))

    def count_ge(v):
        acc, = fold_keys((0.0,), lambda a, s, vb: (a[0] + jnp.where(s >= vb, 1.0, 0.0),), (v,))
        return jnp.sum(acc, axis=0, keepdims=True)

    @pl.when(i * Q_TILE + Q_TILE <= topk)
    def _():
        tau_ref[...] = jnp.full((1, Q_TILE), -jnp.inf, f32)

    @pl.when(i * Q_TILE + Q_TILE > topk)
    def _():
        lo = jnp.min(mn, axis=0, keepdims=True)
        top = jnp.max(mx, axis=0, keepdims=True)
        hi = top + jnp.maximum(jnp.abs(top) * 1e-3, 1e-30)
        n_lo = (qpos + 1).astype(f32)

        def bisect(_, state):
            lo, hi, n_lo = state
            mid = 0.5 * lo + 0.5 * hi
            n_mid = count_ge(mid)
            ge = n_mid >= kf
            return jnp.where(ge, mid, lo), jnp.where(ge, hi, mid), jnp.where(ge, n_mid, n_lo)

        def candidate(lo, n_lo):
            a, = fold_keys((jnp.inf,), lambda a, s, lob: (
                jnp.minimum(a[0], jnp.where(s >= lob, s, jnp.inf)),), (lo,))
            tau = jnp.min(a, axis=0, keepdims=True)
            n_gt, = fold_keys((0.0,), lambda g, s, taub: (
                g[0] + jnp.where(s > taub, 1.0, 0.0),), (tau,))
            n_gt = jnp.sum(n_gt, axis=0, keepdims=True)
            return tau, n_gt, n_lo - n_gt, jnp.max(jnp.where(n_gt >= kf, 1.0, 0.0))

        lo, hi, n_lo = lax.fori_loop(0, BISECT_STEPS, bisect, (lo, hi, n_lo))

        def refine(state):
            _, hi, n_lo, tau, n_gt, _, _ = state
            nxt, = fold_keys((jnp.inf,), lambda a, s, taub: (
                jnp.minimum(a[0], jnp.where(s > taub, s, jnp.inf)),), (tau,))
            open_row = n_gt >= kf
            lo = jnp.where(open_row, jnp.min(nxt, axis=0, keepdims=True), tau)
            n_lo = jnp.where(open_row, n_gt, n_lo)
            lo, hi, n_lo = lax.fori_loop(0, REFINE_STEPS, bisect, (lo, hi, n_lo))
            return (lo, hi, n_lo) + candidate(lo, n_lo)

        _, _, _, tau, n_gt, n_eq, _ = lax.while_loop(lambda st: st[6] > 0.0, refine,
                                                     (lo, hi, n_lo) + candidate(lo, n_lo))
        tau_ref[...] = jnp.where(qpos < topk, -jnp.inf, tau)

        need = kf - n_gt
        surplus = n_eq > need

        @pl.when(jnp.max(jnp.where(surplus, 1.0, 0.0)) > 0.0)
        def _():
            ra = lax.broadcasted_iota(jnp.int32, (IDX_CHUNK, IDX_CHUNK), 0)
            rb = lax.broadcasted_iota(jnp.int32, (IDX_CHUNK, IDX_CHUNK), 1)
            earlier = jnp.where(rb < ra, 1.0, 0.0).astype(bf16)

            def drop(j, seen):
                k0 = pl.multiple_of(j * IDX_CHUNK, IDX_CHUNK)
                s = st_ref[pl.ds(k0, IDX_CHUNK), :]
                eq = jnp.where(s == tau, 1.0, 0.0)
                rank = jnp.dot(earlier, eq.astype(bf16), preferred_element_type=f32) + seen
                st_ref[pl.ds(k0, IDX_CHUNK), :] = jnp.where(
                    s == tau, jnp.where(rank >= need, -jnp.inf, s), s)
                return seen + jnp.sum(eq, axis=0, keepdims=True)

            lax.fori_loop(0, n_idx, drop, jnp.zeros((1, Q_TILE), f32))

    tau = tau_ref[...]
    eye = (lax.broadcasted_iota(jnp.int32, (Q_SUB, Q_SUB), 0)
           == lax.broadcasted_iota(jnp.int32, (Q_SUB, Q_SUB), 1))
    neg_eye = jnp.concatenate([jnp.where(eye, NEG_BIG, 0.0).astype(bf16)] * N_ATT_HEADS, axis=1)
    q_aug = [jnp.concatenate([qabst_ref[u, h * KV_LATENT:(h + 1) * KV_LATENT, :]
                              for h in range(N_ATT_HEADS)], axis=1) for u in range(N_SUB)]
    q_aug = [jnp.concatenate([q, neg_eye], axis=0) for q in q_aug]
    sub_cols = N_ATT_HEADS * Q_SUB

    def stage1(dst, j):
        k0 = pl.multiple_of(j * KV_CHUNK, KV_CHUNK)
        s = st_ref[pl.ds(k0, KV_CHUNK), :]
        kpos = k0 + lax.broadcasted_iota(jnp.int32, (KV_CHUNK, 1), 0)
        dropped = jnp.where(kpos <= qpos, jnp.where(s >= tau, 0.0, 1.0), 1.0).astype(bf16)
        cc = c_ref[pl.ds(k0, KV_CHUNK), :]
        for u in range(N_SUB):
            lhs = jnp.concatenate([cc, dropped[:, u * Q_SUB:(u + 1) * Q_SUB]], axis=1)
            sc = jnp.dot(lhs, q_aug[u], preferred_element_type=f32)
            dst[:, u * sub_cols:(u + 1) * sub_cols] = sc
            mx_ref[:, u * sub_cols:(u + 1) * sub_cols] = jnp.max(sc, axis=0, keepdims=True)

    def stage2(src, j, mx):
        k0 = pl.multiple_of(j * KV_CHUNK, KV_CHUNK)
        m_old = m_ref[...]
        m_new = jnp.maximum(m_old, mx)
        scale = jnp.exp2(m_old - m_new)
        p = jnp.exp2(src[...] - m_new)
        m_ref[...] = m_new
        pv = jnp.dot(ct_ref[:, pl.ds(k0, KV_CHUNK)], p.astype(bf16), preferred_element_type=f32)
        acc_ref[...] = scale * acc_ref[...] + pv

    def step(src, dst, t):
        mx = mx_ref[...]
        stage1(dst, t + 1)
        stage2(src, t, mx)

    m_ref[...] = jnp.full(m_ref.shape, NEG_BIG, f32)
    acc_ref[...] = jnp.zeros(acc_ref.shape, f32)
    stage1(sca_ref, 0)
    n_steps = n_kv - 1

    bufs = (sca_ref, scb_ref)

    def steps(t0, n):
        for r in range(n):
            step(bufs[r % 2], bufs[(r + 1) % 2], t0 + r)

    def unrolled(u, carry):
        steps(STEP_UNROLL * u, STEP_UNROLL)
        return carry

    lax.fori_loop(0, n_steps // STEP_UNROLL, unrolled, 0)
    for rem in range(STEP_UNROLL):
        @pl.when(n_steps % STEP_UNROLL == rem)
        def _(rem=rem):
            steps(n_steps - rem, rem)
            stage2(bufs[rem % 2], n_steps, mx_ref[...])

    o_lat = acc_ref[:KV_LATENT, :] / acc_ref[KV_LATENT:KV_LATENT + 1, :]
    for u in range(N_SUB):
        for g in range(N_ATT_HEADS // 2):
            col = u * sub_cols + 2 * g * Q_SUB
            pair = jnp.concatenate([o_lat[:, col:col + Q_SUB], o_lat[:, col + Q_SUB:col + 2 * Q_SUB]],
                                   axis=0).astype(bf16)
            att_t = jnp.dot(wuvt_ref[g], pair, preferred_element_type=f32)
            o_ref[u * Q_SUB:(u + 1) * Q_SUB, g * LANES:(g + 1) * LANES] = att_t.T.astype(bf16)


def _attn(qabst, qit, wit, ki, c, ct, wuvt2, batch, seq, topk):
    T = batch * seq
    nq = seq // Q_TILE

    def qcol(n):
        return pl.BlockSpec((N_SUB, n, Q_SUB), lambda b, i: (b * nq + i, 0, 0))

    return pl.pallas_call(
        functools.partial(_attn_kernel, topk),
        grid=(batch, nq),
        in_specs=[qcol(N_ATT_HEADS * KV_LATENT), qcol(IDX_HEADS * IDX_DIM), qcol(SUBLANES),
                  pl.BlockSpec((seq, IDX_DIM), lambda b, i: (b, 0)),
                  pl.BlockSpec((seq, KV_LATENT), lambda b, i: (b, 0)),
                  pl.BlockSpec((KV_LATENT + SUM_ROWS, seq), lambda b, i: (0, b)),
                  pl.BlockSpec(wuvt2.shape, lambda b, i: (0, 0, 0), pipeline_mode=pl.Buffered(1))],
        out_specs=pl.BlockSpec((Q_TILE, ATT_WIDTH), lambda b, i: (b * nq + i, 0)),
        out_shape=jax.ShapeDtypeStruct((T, ATT_WIDTH), bf16),
        scratch_shapes=[pltpu.VMEM((seq, Q_TILE), f32),
                        pltpu.VMEM((1, Q_TILE), f32),
                        pltpu.VMEM((KV_CHUNK, N_COLS), f32),
                        pltpu.VMEM((KV_CHUNK, N_COLS), f32),
                        pltpu.VMEM((KV_LATENT + SUM_ROWS, N_COLS), f32),
                        pltpu.VMEM((1, N_COLS), f32),
                        pltpu.VMEM((1, N_COLS), f32)],
        compiler_params=pltpu.CompilerParams(dimension_semantics=("arbitrary", "arbitrary"),
                                             vmem_limit_bytes=VMEM_LIMIT),
        name="attn",
    )(qabst, qit, wit, ki, c, ct, wuvt2)


def _out_ffn_ln_kernel(alpha, h_ref, att_ref, gm_ref, wo_ref, g2_ref, b2_ref,
                       w1_ref, w3_ref, w2_ref, g3_ref, b3_ref, o_ref):
    for r0 in range(0, FFN_TILE, FFN_ROWS):
        rows = slice(r0, r0 + FFN_ROWS)
        mix = jnp.concatenate([att_ref[rows, :], gm_ref[rows, :]], axis=1)
        m = jnp.dot(mix, wo_ref[...], preferred_element_type=f32)
        x = _layernorm(alpha * h_ref[rows, :] + m, g2_ref[...], b2_ref[...])
        y = alpha * x + 0.5 * _swiglu(x.astype(bf16), w1_ref, w3_ref, w2_ref)
        o_ref[rows, :] = _layernorm(y, g3_ref[...], b3_ref[...])


def _out_ffn_ln(h, att, gm, wo, g2, b2, w1, w3, w2, g3, b3, alpha):
    T = h.shape[0]
    row = pl.BlockSpec((FFN_TILE, D_MODEL), lambda i: (i, 0))
    half = pl.BlockSpec((FFN_TILE, ATT_WIDTH), lambda i: (i, 0))
    consts = [wo, g2, b2, w1, w3, w2, g3, b3]
    return pl.pallas_call(
        functools.partial(_out_ffn_ln_kernel, alpha),
        grid=(T // FFN_TILE,),
        in_specs=[row, half, half] + [_const_spec(a.shape) for a in consts],
        out_specs=row,
        out_shape=jax.ShapeDtypeStruct((T, D_MODEL), f32),
        compiler_params=pltpu.CompilerParams(dimension_semantics=("arbitrary",),
                                             vmem_limit_bytes=VMEM_LIMIT),
        name="out_ffn_ln",
    )(h, att, gm, *consts)


def _pack_w_in(w_in):
    splits = np.cumsum([ATT_WIDTH, KV_LATENT, IDX_HEADS * IDX_DIM, IDX_DIM, IDX_HEADS]).tolist()
    q, c, qi, ki, wi, z = jnp.split(w_in, splits, axis=-1)
    pad = jnp.zeros((w_in.shape[0], LANES - IDX_DIM - IDX_HEADS), w_in.dtype)
    return jnp.concatenate([q, c, qi, ki, wi, pad, z], axis=-1).astype(bf16)


def _pair_blockdiag(w):
    H, r, c = w.shape
    z = jnp.zeros((H // 2, r, c), w.dtype)
    top = jnp.concatenate([w[0::2], z], axis=2)
    bot = jnp.concatenate([z, w[1::2]], axis=2)
    return jnp.concatenate([top, bot], axis=1)


def kernel(x, ffn1_w1, ffn1_w3, ffn1_w2, ln1_g, ln1_b, w_in, kv_norm_g, w_uk, w_uv, gmlp_ln_g, gmlp_ln_b, gmlp_ws, gmlp_bs, w_out, ln2_g, ln2_b, ffn2_w1, ffn2_w3, ffn2_w2, ln3_g, ln3_b):
    B, L, D = x.shape
    depth = ffn1_w1.shape[0]
    alpha = float((2 * depth) ** 0.25)
    topk = min(TOPK_MAX, L // 4)
    assert D == D_MODEL and L % IDX_CHUNK == 0 and (B * L) % FFN_TILE == 0
    assert IDX_CHUNK % KV_CHUNK == 0 and IDX_CHUNK == Q_TILE and L % Q_TILE == 0
    assert max(Q_TILE, KV_CHUNK) % min(Q_TILE, KV_CHUNK) == 0
    assert PROJ_TILE % CHUNK == 0 and IDX_CHUNK % FOLD_ROWS == 0

    def vec(a):
        return a.reshape(1, -1)

    h = x.reshape(B * L, D)
    for l in range(depth):
        h = _ffn_ln(h, ffn1_w1[l].astype(bf16), ffn1_w3[l].astype(bf16), ffn1_w2[l].astype(bf16),
                    vec(ln1_g[l]), vec(ln1_b[l]), alpha)
        wuk2 = _pair_blockdiag(jnp.swapaxes(w_uk[l], 1, 2)).astype(bf16)
        wuvt2 = _pair_blockdiag(jnp.swapaxes(w_uv[l], 1, 2)).astype(bf16)
        bs_full = jnp.repeat(gmlp_bs[l].T, MLP_GROUP_DIM, axis=1)
        qabst, c, ct, qit, ki, wit, gm = _proj(
            h, _pack_w_in(w_in[l]), vec(kv_norm_g[l]), wuk2, vec(gmlp_ln_g[l]), vec(gmlp_ln_b[l]),
            gmlp_ws[l], bs_full)
        att = _attn(qabst, qit, wit, ki, c, ct, wuvt2, B, L, topk)
        h = _out_ffn_ln(h, att, gm, w_out[l].astype(bf16), vec(ln2_g[l]), vec(ln2_b[l]),
                        ffn2_w1[l].astype(bf16), ffn2_w3[l].astype(bf16), ffn2_w2[l].astype(bf16),
                        vec(ln3_g[l]), vec(ln3_b[l]), alpha)
    return h.reshape(B, L, D)
```

```python
import functools

import jax
import jax.numpy as jnp
import numpy as np
from jax import lax
from jax.experimental import pallas as pl
from jax.experimental.pallas import tpu as pltpu

D_MODEL = 1024
N_ATT_HEADS = 8
ATT_HEAD_DIM = 64
ATT_WIDTH = N_ATT_HEADS * ATT_HEAD_DIM
KV_LATENT = 128
IDX_HEADS = 4
IDX_DIM = 64
TOPK_MAX = 256
N_MLP_GROUPS = 8
MLP_GROUP_DIM = 64
MLP_WIDTH = N_MLP_GROUPS * MLP_GROUP_DIM
CHUNK = 128
D_FF = 2816
LN_EPS = 1e-5

LANES = 128
SUBLANES = 8
Q_SUB = LANES
N_SUB = 4
Q_TILE = N_SUB * Q_SUB
N_COLS = N_ATT_HEADS * Q_TILE
IDX_CHUNK = 512
KV_CHUNK = 512
STEP_UNROLL = 2
FOLD_ROWS = 64
FFN_TILE = 512
FFN_ROWS = 256
FF_CHUNK = 2816
PROJ_TILE = 512
BISECT_STEPS = 22
REFINE_STEPS = 1
VMEM_LIMIT = 56 * 1024 * 1024
NEG_BIG = -(2.0 ** 100)
LOG2E = 1.4426950408889634
SUM_ROWS = 16

_PC_Q = 0
_PC_C = ATT_WIDTH
_PC_QI = _PC_C + KV_LATENT
_PC_KW = _PC_QI + IDX_HEADS * IDX_DIM
_PC_Z = _PC_KW + LANES
_PC_END = _PC_Z + 2 * MLP_WIDTH

bf16 = jnp.bfloat16
f32 = jnp.float32


def _const_spec(shape):
    nd = len(shape)
    return pl.BlockSpec(shape, lambda *_: (0,) * nd, pipeline_mode=pl.Buffered(1))


def _layernorm(y, g, b):
    mu = jnp.mean(y, axis=-1, keepdims=True)
    d = y - mu
    var = jnp.mean(d * d, axis=-1, keepdims=True)
    return d * lax.rsqrt(var + LN_EPS) * g + b


def _swiglu(xb, w1_ref, w3_ref, w2_ref):
    acc = None
    for c0 in range(0, D_FF, FF_CHUNK):
        a = jnp.dot(xb, w1_ref[:, c0:c0 + FF_CHUNK], preferred_element_type=f32)
        b = jnp.dot(xb, w3_ref[:, c0:c0 + FF_CHUNK], preferred_element_type=f32)
        h = (jax.nn.silu(a) * b).astype(bf16)
        part = jnp.dot(h, w2_ref[c0:c0 + FF_CHUNK, :], preferred_element_type=f32)
        acc = part if acc is None else acc + part
    return acc


def _ffn_ln_kernel(alpha, x_ref, w1_ref, w3_ref, w2_ref, g_ref, b_ref, o_ref):
    for r0 in range(0, FFN_TILE, FFN_ROWS):
        x = x_ref[r0:r0 + FFN_ROWS, :]
        y = alpha * x + 0.5 * _swiglu(x.astype(bf16), w1_ref, w3_ref, w2_ref)
        o_ref[r0:r0 + FFN_ROWS, :] = _layernorm(y, g_ref[...], b_ref[...])


def _ffn_ln(x, w1, w3, w2, g, b, alpha):
    T = x.shape[0]
    row = pl.BlockSpec((FFN_TILE, D_MODEL), lambda i: (i, 0))
    return pl.pallas_call(
        functools.partial(_ffn_ln_kernel, alpha),
        grid=(T // FFN_TILE,),
        in_specs=[row, _const_spec(w1.shape), _const_spec(w3.shape), _const_spec(w2.shape),
                  _const_spec(g.shape), _const_spec(b.shape)],
        out_specs=row,
        out_shape=jax.ShapeDtypeStruct((T, D_MODEL), f32),
        compiler_params=pltpu.CompilerParams(dimension_semantics=("arbitrary",),
                                             vmem_limit_bytes=VMEM_LIMIT),
        name="ffn_ln",
    )(x, w1, w3, w2, g, b)


def _proj_kernel(h_ref, wp_ref, kvg_ref, wuk_ref, glng_ref, glnb_ref, ws_ref, bs_ref,
                 qabst_ref, c_ref, ct_ref, qit_ref, ki_ref, wit_ref, gm_ref):
    hb = h_ref[...].astype(bf16)
    proj = jnp.dot(hb, wp_ref[...], preferred_element_type=f32)

    for j in range(N_ATT_HEADS // 2):
        qj = proj[:, _PC_Q + j * LANES:_PC_Q + (j + 1) * LANES].astype(bf16)
        qa = jnp.dot(qj, wuk_ref[j], preferred_element_type=f32) * (ATT_HEAD_DIM ** -0.5 * LOG2E)
        for t in range(PROJ_TILE // Q_SUB):
            qabst_ref[t, 2 * j * KV_LATENT:2 * (j + 1) * KV_LATENT, :] = (
                qa[t * Q_SUB:(t + 1) * Q_SUB, :].T.astype(bf16))

    c = proj[:, _PC_C:_PC_C + KV_LATENT]
    c = c * lax.rsqrt(jnp.mean(c * c, axis=-1, keepdims=True) + LN_EPS) * kvg_ref[...]
    c_ref[...] = c.astype(bf16)
    ones_row = lax.broadcasted_iota(jnp.int32, (SUM_ROWS, PROJ_TILE), 0) == 0
    ct_ref[...] = jnp.concatenate([c.T, jnp.where(ones_row, 1.0, 0.0)], axis=0).astype(bf16)

    qi = proj[:, _PC_QI:_PC_QI + IDX_HEADS * IDX_DIM]
    kw = proj[:, _PC_KW:_PC_KW + LANES]
    ki_ref[...] = kw[:, :IDX_DIM].astype(bf16)
    for t in range(PROJ_TILE // Q_SUB):
        qit_ref[t] = qi[t * Q_SUB:(t + 1) * Q_SUB, :].T.astype(bf16)
        wit_ref[t] = kw[t * Q_SUB:(t + 1) * Q_SUB, :].T[IDX_DIM:IDX_DIM + SUBLANES, :]

    z = jax.nn.gelu(proj[:, _PC_Z:_PC_END])
    u = z[:, :MLP_WIDTH]
    v = _layernorm(z[:, MLP_WIDTH:], glng_ref[...], glnb_ref[...]).astype(bf16)
    r = lax.broadcasted_iota(jnp.int32, (CHUNK, CHUNK), 0)
    s = lax.broadcasted_iota(jnp.int32, (CHUNK, CHUNK), 1)
    tri = s <= r
    left = lax.broadcasted_iota(jnp.int32, (CHUNK, LANES), 1) < MLP_GROUP_DIM
    for t0 in range(0, PROJ_TILE, CHUNK):
        for j in range(N_MLP_GROUPS // 2):
            vj = v[t0:t0 + CHUNK, j * LANES:(j + 1) * LANES]
            wa = jnp.where(tri, ws_ref[2 * j], 0.0).astype(bf16)
            wb = jnp.where(tri, ws_ref[2 * j + 1], 0.0).astype(bf16)
            sa = jnp.dot(wa, vj, preferred_element_type=f32)
            sb = jnp.dot(wb, vj, preferred_element_type=f32)
            sv = jnp.where(left, sa, sb) + bs_ref[:, j * LANES:(j + 1) * LANES]
            gm_ref[t0:t0 + CHUNK, j * LANES:(j + 1) * LANES] = (
                u[t0:t0 + CHUNK, j * LANES:(j + 1) * LANES] * sv).astype(bf16)


def _proj(h, wp, kvg, wuk2, glng, glnb, ws, bs_full):
    T = h.shape[0]
    tm = PROJ_TILE

    def rows(n, dt):
        return pl.BlockSpec((tm, n), lambda i: (i, 0)), jax.ShapeDtypeStruct((T, n), dt)

    def cols(n, dt):
        return pl.BlockSpec((n, tm), lambda i: (0, i)), jax.ShapeDtypeStruct((n, T), dt)

    def qtiles(n, dt):
        return (pl.BlockSpec((tm // Q_SUB, n, Q_SUB), lambda i: (i, 0, 0)),
                jax.ShapeDtypeStruct((T // Q_SUB, n, Q_SUB), dt))

    outs = [qtiles(N_ATT_HEADS * KV_LATENT, bf16),
            rows(KV_LATENT, bf16),
            cols(KV_LATENT + SUM_ROWS, bf16),
            qtiles(IDX_HEADS * IDX_DIM, bf16),
            rows(IDX_DIM, bf16),
            qtiles(SUBLANES, f32),
            rows(MLP_WIDTH, bf16)]
    return pl.pallas_call(
        _proj_kernel,
        grid=(T // tm,),
        in_specs=[pl.BlockSpec((tm, D_MODEL), lambda i: (i, 0)),
                  _const_spec(wp.shape), _const_spec(kvg.shape), _const_spec(wuk2.shape),
                  _const_spec(glng.shape), _const_spec(glnb.shape), _const_spec(ws.shape),
                  _const_spec(bs_full.shape)],
        out_specs=[o[0] for o in outs],
        out_shape=[o[1] for o in outs],
        compiler_params=pltpu.CompilerParams(dimension_semantics=("arbitrary",),
                                             vmem_limit_bytes=VMEM_LIMIT),
        name="proj",
    )(h, wp, kvg, wuk2, glng, glnb, ws, bs_full)


def _attn_kernel(topk, qabst_ref, qit_ref, wit_ref, ki_ref, c_ref, ct_ref, wuvt_ref, o_ref,
                 st_ref, tau_ref, sca_ref, scb_ref, acc_ref, m_ref, mx_ref):
    i = pl.program_id(1)
    n_keys = (i + 1) * Q_TILE
    n_idx = (n_keys + IDX_CHUNK - 1) // IDX_CHUNK
    n_kv = (n_keys + KV_CHUNK - 1) // KV_CHUNK
    qpos = i * Q_TILE + lax.broadcasted_iota(jnp.int32, (1, Q_TILE), 1)
    kf = float(topk)
    fold_shape = (FOLD_ROWS, Q_TILE)

    def fold_init(v):
        return jnp.full(fold_shape, v, f32)

    def fold_blocks(acc, s, fn):
        for t in range(0, s.shape[0], FOLD_ROWS):
            acc = fn(acc, s[t:t + FOLD_ROWS])
        return acc

    qcat = jnp.concatenate([qit_ref[u, h * IDX_DIM:(h + 1) * IDX_DIM, :]
                            for u in range(N_SUB) for h in range(IDX_HEADS)], axis=1)
    wit = [wit_ref[u] * (IDX_HEADS ** -0.5 * IDX_DIM ** -0.5) for u in range(N_SUB)]

    def scores(j):
        k0 = pl.multiple_of(j * IDX_CHUNK, IDX_CHUNK)
        lg = jnp.dot(ki_ref[pl.ds(k0, IDX_CHUNK), :], qcat, preferred_element_type=f32)
        parts = []
        for u in range(N_SUB):
            s = None
            for h in range(IDX_HEADS):
                col = (u * IDX_HEADS + h) * Q_SUB
                term = wit[u][h:h + 1, :] * jnp.maximum(lg[:, col:col + Q_SUB], 0.0)
                s = term if s is None else s + term
            parts.append(s)
        return k0, jnp.concatenate(parts, axis=1)

    def causal_chunk(j, carry):
        mn, mx = carry
        k0, s = scores(j)
        st_ref[pl.ds(k0, IDX_CHUNK), :] = s
        return fold_blocks(mn, s, jnp.minimum), fold_blocks(mx, s, jnp.maximum)

    def causal_pair(u, carry):
        return causal_chunk(2 * u + 1, causal_chunk(2 * u, carry))

    n_full = n_idx - 1
    mn, mx = lax.fori_loop(0, n_full // 2, causal_pair, (fold_init(jnp.inf), fold_init(-jnp.inf)))
    mn, mx = lax.cond(n_full % 2 == 1, lambda c: causal_chunk(n_full - 1, c), lambda c: c, (mn, mx))
    k0, s = scores(n_idx - 1)
    causal = k0 + lax.broadcasted_iota(jnp.int32, (IDX_CHUNK, 1), 0) <= qpos
    st_ref[pl.ds(k0, IDX_CHUNK), :] = jnp.where(causal, s, -jnp.inf)
    mn = fold_blocks(mn, jnp.where(causal, s, jnp.inf), jnp.minimum)
    mx = fold_blocks(mx, jnp.where(causal, s, -jnp.inf), jnp.maximum)

    def fold_keys(inits, fn, rows):
        full = tuple(jnp.broadcast_to(r, fold_shape) for r in rows)

        def body(j, accs):
            k0 = pl.multiple_of(j * IDX_CHUNK, IDX_CHUNK)
            for t in range(0, IDX_CHUNK, FOLD_ROWS):
                accs = fn(accs, st_ref[pl.ds(k0 + t, FOLD_ROWS), :], *full)
            return accs

        accs = lax.fori_loop(0, n_idx - 1, body, tuple(fold_init(v) for v in inits))
        k0 = pl.multiple_of((n_idx - 1) * IDX_CHUNK, IDX_CHUNK)
        parts = []
        for u in range(N_SUB):
            cols = slice(u * Q_SUB, (u + 1) * Q_SUB)
            part = tuple(a[:, cols] for a in accs)
            sub = tuple(f[:, cols] for f in full)
            for t in range(0, (u + 1) * Q_SUB, FOLD_ROWS):
                part = fn(part, st_ref[pl.ds(k0 + t, FOLD_ROWS), cols], *sub)
            parts.append(part)
        return tuple(jnp.concatenate([p[a] for p in parts], axis=1) for a in range(len(inits)))

    def count_ge(v):
        acc, = fold_keys((0.0,), lambda a, s, vb: (a[0] + jnp.where(s >= vb, 1.0, 0.0),), (v,))
        return jnp.sum(acc, axis=0, keepdims=True)

    @pl.when(i * Q_TILE + Q_TILE <= topk)
    def _():
        tau_ref[...] = jnp.full((1, Q_TILE), -jnp.inf, f32)

    @pl.when(i * Q_TILE + Q_TILE > topk)
    def _():
        lo = jnp.min(mn, axis=0, keepdims=True)
        top = jnp.max(mx, axis=0, keepdims=True)
        hi = top + jnp.maximum(jnp.abs(top) * 1e-3, 1e-30)
        n_lo = (qpos + 1).astype(f32)

        def bisect(_, state):
            lo, hi, n_lo = state
            mid = 0.5 * lo + 0.5 * hi
            n_mid = count_ge(mid)
            ge = n_mid >= kf
            return jnp.where(ge, mid, lo), jnp.where(ge, hi, mid), jnp.where(ge, n_mid, n_lo)

        def candidate(lo, n_lo):
            a, = fold_keys((jnp.inf,), lambda a, s, lob: (
                jnp.minimum(a[0], jnp.where(s >= lob, s, jnp.inf)),), (lo,))
            tau = jnp.min(a, axis=0, keepdims=True)
            n_gt, = fold_keys((0.0,), lambda g, s, taub: (
                g[0] + jnp.where(s > taub, 1.0, 0.0),), (tau,))
            n_gt = jnp.sum(n_gt, axis=0, keepdims=True)
            return tau, n_gt, n_lo - n_gt, jnp.max(jnp.where(n_gt >= kf, 1.0, 0.0))

        lo, hi, n_lo = lax.fori_loop(0, BISECT_STEPS, bisect, (lo, hi, n_lo))

        def refine(state):
            _, hi, n_lo, tau, n_gt, _, _ = state
            nxt, = fold_keys((jnp.inf,), lambda a, s, taub: (
                jnp.minimum(a[0], jnp.where(s > taub, s, jnp.inf)),), (tau,))
            open_row = n_gt >= kf
            lo = jnp.where(open_row, jnp.min(nxt, axis=0, keepdims=True), tau)
            n_lo = jnp.where(open_row, n_gt, n_lo)
            lo, hi, n_lo = lax.fori_loop(0, REFINE_STEPS, bisect, (lo, hi, n_lo))
            return (lo, hi, n_lo) + candidate(lo, n_lo)

        _, _, _, tau, n_gt, n_eq, _ = lax.while_loop(lambda st: st[6] > 0.0, refine,
                                                     (lo, hi, n_lo) + candidate(lo, n_lo))
        tau_ref[...] = jnp.where(qpos < topk, -jnp.inf, tau)

        need = kf - n_gt
        surplus = n_eq > need

        @pl.when(jnp.max(jnp.where(surplus, 1.0, 0.0)) > 0.0)
        def _():
            ra = lax.broadcasted_iota(jnp.int32, (IDX_CHUNK, IDX_CHUNK), 0)
            rb = lax.broadcasted_iota(jnp.int32, (IDX_CHUNK, IDX_CHUNK), 1)
            earlier = jnp.where(rb < ra, 1.0, 0.0).astype(bf16)

            def drop(j, seen):
                k0 = pl.multiple_of(j * IDX_CHUNK, IDX_CHUNK)
                s = st_ref[pl.ds(k0, IDX_CHUNK), :]
                eq = jnp.where(s == tau, 1.0, 0.0)
                rank = jnp.dot(earlier, eq.astype(bf16), preferred_element_type=f32) + seen
                st_ref[pl.ds(k0, IDX_CHUNK), :] = jnp.where(
                    s == tau, jnp.where(rank >= need, -jnp.inf, s), s)
                return seen + jnp.sum(eq, axis=0, keepdims=True)

            lax.fori_loop(0, n_idx, drop, jnp.zeros((1, Q_TILE), f32))

    tau = tau_ref[...]
    eye = (lax.broadcasted_iota(jnp.int32, (Q_SUB, Q_SUB), 0)
           == lax.broadcasted_iota(jnp.int32, (Q_SUB, Q_SUB), 1))
    neg_eye = jnp.concatenate([jnp.where(eye, NEG_BIG, 0.0).astype(bf16)] * N_ATT_HEADS, axis=1)
    q_aug = [jnp.concatenate([qabst_ref[u, h * KV_LATENT:(h + 1) * KV_LATENT, :]
                              for h in range(N_ATT_HEADS)], axis=1) for u in range(N_SUB)]
    q_aug = [jnp.concatenate([q, neg_eye], axis=0) for q in q_aug]
    sub_cols = N_ATT_HEADS * Q_SUB

    def stage1(dst, j):
        k0 = pl.multiple_of(j * KV_CHUNK, KV_CHUNK)
        s = st_ref[pl.ds(k0, KV_CHUNK), :]
        kpos = k0 + lax.broadcasted_iota(jnp.int32, (KV_CHUNK, 1), 0)
        dropped = jnp.where(kpos <= qpos, jnp.where(s >= tau, 0.0, 1.0), 1.0).astype(bf16)
        cc = c_ref[pl.ds(k0, KV_CHUNK), :]
        for u in range(N_SUB):
            lhs = jnp.concatenate([cc, dropped[:, u * Q_SUB:(u + 1) * Q_SUB]], axis=1)
            sc = jnp.dot(lhs, q_aug[u], preferred_element_type=f32)
            dst[:, u * sub_cols:(u + 1) * sub_cols] = sc
            mx_ref[:, u * sub_cols:(u + 1) * sub_cols] = jnp.max(sc, axis=0, keepdims=True)

    def stage2(src, j, mx):
        k0 = pl.multiple_of(j * KV_CHUNK, KV_CHUNK)
        m_old = m_ref[...]
        m_new = jnp.maximum(m_old, mx)
        scale = jnp.exp2(m_old - m_new)
        p = jnp.exp2(src[...] - m_new)
        m_ref[...] = m_new
        pv = jnp.dot(ct_ref[:, pl.ds(k0, KV_CHUNK)], p.astype(bf16), preferred_element_type=f32)
        acc_ref[...] = scale * acc_ref[...] + pv

    def stage2_own(src, j, mx):
        k0 = pl.multiple_of(j * KV_CHUNK, KV_CHUNK)
        m_old = m_ref[...]
        m_new = jnp.maximum(m_old, mx)
        scale = jnp.exp2(m_old - m_new)
        m_ref[...] = m_new
        for u in range(N_SUB):
            rows = (u + 1) * Q_SUB
            cols = slice(u * sub_cols, (u + 1) * sub_cols)
            p = jnp.exp2(src[:rows, cols] - m_new[:, cols])
            pv = jnp.dot(ct_ref[:, pl.ds(k0, rows)], p.astype(bf16), preferred_element_type=f32)
            acc_ref[:, cols] = scale[:, cols] * acc_ref[:, cols] + pv

    def step(src, dst, t):
        mx = mx_ref[...]
        stage1(dst, t + 1)
        stage2(src, t, mx)

    m_ref[...] = jnp.full(m_ref.shape, NEG_BIG, f32)
    acc_ref[...] = jnp.zeros(acc_ref.shape, f32)
    stage1(sca_ref, 0)
    n_steps = n_kv - 1

    bufs = (sca_ref, scb_ref)

    def steps(t0, n):
        for r in range(n):
            step(bufs[r % 2], bufs[(r + 1) % 2], t0 + r)

    def unrolled(u, carry):
        steps(STEP_UNROLL * u, STEP_UNROLL)
        return carry

    lax.fori_loop(0, n_steps // STEP_UNROLL, unrolled, 0)
    for rem in range(STEP_UNROLL):
        @pl.when(n_steps % STEP_UNROLL == rem)
        def _(rem=rem):
            steps(n_steps - rem, rem)
            stage2_own(bufs[rem % 2], n_steps, mx_ref[...])

    o_lat = acc_ref[:KV_LATENT, :] / acc_ref[KV_LATENT:KV_LATENT + 1, :]
    for u in range(N_SUB):
        for g in range(N_ATT_HEADS // 2):
            col = u * sub_cols + 2 * g * Q_SUB
            pair = jnp.concatenate([o_lat[:, col:col + Q_SUB], o_lat[:, col + Q_SUB:col + 2 * Q_SUB]],
                                   axis=0).astype(bf16)
            att_t = jnp.dot(wuvt_ref[g], pair, preferred_element_type=f32)
            o_ref[u * Q_SUB:(u + 1) * Q_SUB, g * LANES:(g + 1) * LANES] = att_t.T.astype(bf16)


def _attn(qabst, qit, wit, ki, c, ct, wuvt2, batch, seq, topk):
    T = batch * seq
    nq = seq // Q_TILE

    def qcol(n):
        return pl.BlockSpec((N_SUB, n, Q_SUB), lambda b, i: (b * nq + i, 0, 0))

    return pl.pallas_call(
        functools.partial(_attn_kernel, topk),
        grid=(batch, nq),
        in_specs=[qcol(N_ATT_HEADS * KV_LATENT), qcol(IDX_HEADS * IDX_DIM), qcol(SUBLANES),
                  pl.BlockSpec((seq, IDX_DIM), lambda b, i: (b, 0)),
                  pl.BlockSpec((seq, KV_LATENT), lambda b, i: (b, 0)),
                  pl.BlockSpec((KV_LATENT + SUM_ROWS, seq), lambda b, i: (0, b)),
                  pl.BlockSpec(wuvt2.shape, lambda b, i: (0, 0, 0), pipeline_mode=pl.Buffered(1))],
        out_specs=pl.BlockSpec((Q_TILE, ATT_WIDTH), lambda b, i: (b * nq + i, 0)),
        out_shape=jax.ShapeDtypeStruct((T, ATT_WIDTH), bf16),
        scratch_shapes=[pltpu.VMEM((seq, Q_TILE), f32),
                        pltpu.VMEM((1, Q_TILE), f32),
                        pltpu.VMEM((KV_CHUNK, N_COLS), f32),
                        pltpu.VMEM((KV_CHUNK, N_COLS), f32),
                        pltpu.VMEM((KV_LATENT + SUM_ROWS, N_COLS), f32),
                        pltpu.VMEM((1, N_COLS), f32),
                        pltpu.VMEM((1, N_COLS), f32)],
        compiler_params=pltpu.CompilerParams(dimension_semantics=("arbitrary", "arbitrary"),
                                             vmem_limit_bytes=VMEM_LIMIT),
        name="attn",
    )(qabst, qit, wit, ki, c, ct, wuvt2)


def _out_ffn_ln_kernel(alpha, h_ref, att_ref, gm_ref, wo_ref, g2_ref, b2_ref,
                       w1_ref, w3_ref, w2_ref, g3_ref, b3_ref, o_ref):
    for r0 in range(0, FFN_TILE, FFN_ROWS):
        rows = slice(r0, r0 + FFN_ROWS)
        mix = jnp.concatenate([att_ref[rows, :], gm_ref[rows, :]], axis=1)
        m = jnp.dot(mix, wo_ref[...], preferred_element_type=f32)
        x = _layernorm(alpha * h_ref[rows, :] + m, g2_ref[...], b2_ref[...])
        y = alpha * x + 0.5 * _swiglu(x.astype(bf16), w1_ref, w3_ref, w2_ref)
        o_ref[rows, :] = _layernorm(y, g3_ref[...], b3_ref[...])


def _out_ffn_ln(h, att, gm, wo, g2, b2, w1, w3, w2, g3, b3, alpha):
    T = h.shape[0]
    row = pl.BlockSpec((FFN_TILE, D_MODEL), lambda i: (i, 0))
    half = pl.BlockSpec((FFN_TILE, ATT_WIDTH), lambda i: (i, 0))
    consts = [wo, g2, b2, w1, w3, w2, g3, b3]
    return pl.pallas_call(
        functools.partial(_out_ffn_ln_kernel, alpha),
        grid=(T // FFN_TILE,),
        in_specs=[row, half, half] + [_const_spec(a.shape) for a in consts],
        out_specs=row,
        out_shape=jax.ShapeDtypeStruct((T, D_MODEL), f32),
        compiler_params=pltpu.CompilerParams(dimension_semantics=("arbitrary",),
                                             vmem_limit_bytes=VMEM_LIMIT),
        name="out_ffn_ln",
    )(h, att, gm, *consts)


def _pack_w_in(w_in):
    splits = np.cumsum([ATT_WIDTH, KV_LATENT, IDX_HEADS * IDX_DIM, IDX_DIM, IDX_HEADS]).tolist()
    q, c, qi, ki, wi, z = jnp.split(w_in, splits, axis=-1)
    pad = jnp.zeros((w_in.shape[0], LANES - IDX_DIM - IDX_HEADS), w_in.dtype)
    return jnp.concatenate([q, c, qi, ki, wi, pad, z], axis=-1).astype(bf16)


def _pair_blockdiag(w):
    H, r, c = w.shape
    z = jnp.zeros((H // 2, r, c), w.dtype)
    top = jnp.concatenate([w[0::2], z], axis=2)
    bot = jnp.concatenate([z, w[1::2]], axis=2)
    return jnp.concatenate([top, bot], axis=1)


def kernel(x, ffn1_w1, ffn1_w3, ffn1_w2, ln1_g, ln1_b, w_in, kv_norm_g, w_uk, w_uv, gmlp_ln_g, gmlp_ln_b, gmlp_ws, gmlp_bs, w_out, ln2_g, ln2_b, ffn2_w1, ffn2_w3, ffn2_w2, ln3_g, ln3_b):
    B, L, D = x.shape
    depth = ffn1_w1.shape[0]
    alpha = float((2 * depth) ** 0.25)
    topk = min(TOPK_MAX, L // 4)
    assert D == D_MODEL and L % IDX_CHUNK == 0 and (B * L) % FFN_TILE == 0
    assert KV_CHUNK == Q_TILE and IDX_CHUNK == Q_TILE and L % Q_TILE == 0
    assert max(Q_TILE, KV_CHUNK) % min(Q_TILE, KV_CHUNK) == 0
    assert PROJ_TILE % CHUNK == 0 and IDX_CHUNK % FOLD_ROWS == 0

    def vec(a):
        return a.reshape(1, -1)

    h = x.reshape(B * L, D)
    for l in range(depth):
        h = _ffn_ln(h, ffn1_w1[l].astype(bf16), ffn1_w3[l].astype(bf16), ffn1_w2[l].astype(bf16),
                    vec(ln1_g[l]), vec(ln1_b[l]), alpha)
        wuk2 = _pair_blockdiag(jnp.swapaxes(w_uk[l], 1, 2)).astype(bf16)
        wuvt2 = _pair_blockdiag(jnp.swapaxes(w_uv[l], 1, 2)).astype(bf16)
        bs_full = jnp.repeat(gmlp_bs[l].T, MLP_GROUP_DIM, axis=1)
        qabst, c, ct, qit, ki, wit, gm = _proj(
            h, _pack_w_in(w_in[l]), vec(kv_norm_g[l]), wuk2, vec(gmlp_ln_g[l]), vec(gmlp_ln_b[l]),
            gmlp_ws[l], bs_full)
        att = _attn(qabst, qit, wit, ki, c, ct, wuvt2, B, L, topk)
        h = _out_ffn_ln(h, att, gm, w_out[l].astype(bf16), vec(ln2_g[l]), vec(ln2_b[l]),
                        ffn2_w1[l].astype(bf16), ffn2_w3[l].astype(bf16), ffn2_w2[l].astype(bf16),
                        vec(ln3_g[l]), vec(ln3_b[l]), alpha)
    return h.reshape(B, L, D)
```

```python
import functools

import jax
import jax.numpy as jnp
import numpy as np
from jax import lax
from jax.experimental import pallas as pl
from jax.experimental.pallas import tpu as pltpu

D_MODEL = 1024
N_ATT_HEADS = 8
ATT_HEAD_DIM = 64
ATT_WIDTH = N_ATT_HEADS * ATT_HEAD_DIM
KV_LATENT = 128
IDX_HEADS = 4
IDX_DIM = 64
TOPK_MAX = 256
N_MLP_GROUPS = 8
MLP_GROUP_DIM = 64
MLP_WIDTH = N_MLP_GROUPS * MLP_GROUP_DIM
CHUNK = 128
D_FF = 2816
LN_EPS = 1e-5

LANES = 128
SUBLANES = 8
Q_SUB = LANES
N_SUB = 4
Q_TILE = N_SUB * Q_SUB
N_COLS = N_ATT_HEADS * Q_TILE
IDX_CHUNK = 512
KV_CHUNK = 512
STEP_UNROLL = 2
FOLD_ROWS = 64
FFN_TILE = 512
FFN_ROWS = 256
FF_CHUNK = 2816
PROJ_TILE = 512
BISECT_STEPS = 22
VMEM_LIMIT = 56 * 1024 * 1024
NEG_BIG = -(2.0 ** 100)
LOG2E = 1.4426950408889634
SUM_ROWS = 16

_PC_Q = 0
_PC_C = ATT_WIDTH
_PC_QI = _PC_C + KV_LATENT
_PC_KW = _PC_QI + IDX_HEADS * IDX_DIM
_PC_Z = _PC_KW + LANES

bf16 = jnp.bfloat16
f32 = jnp.float32


def _const_spec(shape):
    nd = len(shape)
    return pl.BlockSpec(shape, lambda *_: (0,) * nd, pipeline_mode=pl.Buffered(1))


def _layernorm(y, g, b):
    mu = jnp.mean(y, axis=-1, keepdims=True)
    d = y - mu
    var = jnp.mean(d * d, axis=-1, keepdims=True)
    return d * lax.rsqrt(var + LN_EPS) * g + b


def _swiglu(xb, w1_ref, w3_ref, w2_ref):
    acc = None
    for c0 in range(0, D_FF, FF_CHUNK):
        a = jnp.dot(xb, w1_ref[:, c0:c0 + FF_CHUNK], preferred_element_type=f32)
        b = jnp.dot(xb, w3_ref[:, c0:c0 + FF_CHUNK], preferred_element_type=f32)
        h = (jax.nn.silu(a) * b).astype(bf16)
        part = jnp.dot(h, w2_ref[c0:c0 + FF_CHUNK, :], preferred_element_type=f32)
        acc = part if acc is None else acc + part
    return acc


def _ffn_ln_kernel(alpha, x_ref, w1_ref, w3_ref, w2_ref, g_ref, b_ref, o_ref):
    for r0 in range(0, FFN_TILE, FFN_ROWS):
        x = x_ref[r0:r0 + FFN_ROWS, :]
        y = alpha * x + 0.5 * _swiglu(x.astype(bf16), w1_ref, w3_ref, w2_ref)
        o_ref[r0:r0 + FFN_ROWS, :] = _layernorm(y, g_ref[...], b_ref[...])


def _ffn_ln(x, w1, w3, w2, g, b, alpha):
    T = x.shape[0]
    row = pl.BlockSpec((FFN_TILE, D_MODEL), lambda i: (i, 0))
    return pl.pallas_call(
        functools.partial(_ffn_ln_kernel, alpha),
        grid=(T // FFN_TILE,),
        in_specs=[row, _const_spec(w1.shape), _const_spec(w3.shape), _const_spec(w2.shape),
                  _const_spec(g.shape), _const_spec(b.shape)],
        out_specs=row,
        out_shape=jax.ShapeDtypeStruct((T, D_MODEL), f32),
        compiler_params=pltpu.CompilerParams(dimension_semantics=("arbitrary",),
                                             vmem_limit_bytes=VMEM_LIMIT),
        name="ffn_ln",
    )(x, w1, w3, w2, g, b)


def _proj_kernel(h_ref, wh_ref, wz_ref, kvg_ref, wuk_ref, glng_ref, glnb_ref, ws_ref, bs_ref,
                 qabst_ref, c_ref, ct_ref, qit_ref, ki_ref, wit_ref, gm_ref):
    hb = h_ref[...].astype(bf16)
    proj = jnp.dot(hb, wh_ref[...], preferred_element_type=f32)
    proj_z = jnp.dot(hb, wz_ref[...], preferred_element_type=f32)

    for j in range(N_ATT_HEADS // 2):
        qj = proj[:, _PC_Q + j * LANES:_PC_Q + (j + 1) * LANES].astype(bf16)
        qa = jnp.dot(qj, wuk_ref[j], preferred_element_type=f32) * (ATT_HEAD_DIM ** -0.5 * LOG2E)
        for t in range(PROJ_TILE // Q_SUB):
            qabst_ref[t, 2 * j * KV_LATENT:2 * (j + 1) * KV_LATENT, :] = (
                qa[t * Q_SUB:(t + 1) * Q_SUB, :].T.astype(bf16))

    c = proj[:, _PC_C:_PC_C + KV_LATENT]
    c = c * lax.rsqrt(jnp.mean(c * c, axis=-1, keepdims=True) + LN_EPS) * kvg_ref[...]
    c_ref[...] = c.astype(bf16)
    ones_row = lax.broadcasted_iota(jnp.int32, (SUM_ROWS, PROJ_TILE), 0) == 0
    ct_ref[...] = jnp.concatenate([c.T, jnp.where(ones_row, 1.0, 0.0)], axis=0).astype(bf16)

    qi = proj[:, _PC_QI:_PC_QI + IDX_HEADS * IDX_DIM]
    kw = proj[:, _PC_KW:_PC_KW + LANES]
    ki_ref[...] = kw[:, :IDX_DIM].astype(bf16)
    for t in range(PROJ_TILE // Q_SUB):
        qit_ref[t] = qi[t * Q_SUB:(t + 1) * Q_SUB, :].T.astype(bf16)
        wit_ref[t] = kw[t * Q_SUB:(t + 1) * Q_SUB, :].T[IDX_DIM:IDX_DIM + SUBLANES, :]

    z = jax.nn.gelu(proj_z)
    u = z[:, :MLP_WIDTH]
    v = _layernorm(z[:, MLP_WIDTH:], glng_ref[...], glnb_ref[...]).astype(bf16)
    r = lax.broadcasted_iota(jnp.int32, (CHUNK, CHUNK), 0)
    s = lax.broadcasted_iota(jnp.int32, (CHUNK, CHUNK), 1)
    tri = s <= r
    left = lax.broadcasted_iota(jnp.int32, (CHUNK, LANES), 1) < MLP_GROUP_DIM
    for t0 in range(0, PROJ_TILE, CHUNK):
        for j in range(N_MLP_GROUPS // 2):
            vj = v[t0:t0 + CHUNK, j * LANES:(j + 1) * LANES]
            wa = jnp.where(tri, ws_ref[2 * j], 0.0).astype(bf16)
            wb = jnp.where(tri, ws_ref[2 * j + 1], 0.0).astype(bf16)
            sa = jnp.dot(wa, vj, preferred_element_type=f32)
            sb = jnp.dot(wb, vj, preferred_element_type=f32)
            sv = jnp.where(left, sa, sb) + bs_ref[:, j * LANES:(j + 1) * LANES]
            gm_ref[t0:t0 + CHUNK, j * LANES:(j + 1) * LANES] = (
                u[t0:t0 + CHUNK, j * LANES:(j + 1) * LANES] * sv).astype(bf16)


def _proj(h, wh, wz, kvg, wuk2, glng, glnb, ws, bs_full):
    T = h.shape[0]
    tm = PROJ_TILE

    def rows(n, dt):
        return pl.BlockSpec((tm, n), lambda i: (i, 0)), jax.ShapeDtypeStruct((T, n), dt)

    def cols(n, dt):
        return pl.BlockSpec((n, tm), lambda i: (0, i)), jax.ShapeDtypeStruct((n, T), dt)

    def qtiles(n, dt):
        return (pl.BlockSpec((tm // Q_SUB, n, Q_SUB), lambda i: (i, 0, 0)),
                jax.ShapeDtypeStruct((T // Q_SUB, n, Q_SUB), dt))

    outs = [qtiles(N_ATT_HEADS * KV_LATENT, bf16),
            rows(KV_LATENT, bf16),
            cols(KV_LATENT + SUM_ROWS, bf16),
            qtiles(IDX_HEADS * IDX_DIM, bf16),
            rows(IDX_DIM, bf16),
            qtiles(SUBLANES, f32),
            rows(MLP_WIDTH, bf16)]
    return pl.pallas_call(
        _proj_kernel,
        grid=(T // tm,),
        in_specs=[pl.BlockSpec((tm, D_MODEL), lambda i: (i, 0)),
                  _const_spec(wh.shape), _const_spec(wz.shape), _const_spec(kvg.shape),
                  _const_spec(wuk2.shape),
                  _const_spec(glng.shape), _const_spec(glnb.shape), _const_spec(ws.shape),
                  _const_spec(bs_full.shape)],
        out_specs=[o[0] for o in outs],
        out_shape=[o[1] for o in outs],
        compiler_params=pltpu.CompilerParams(dimension_semantics=("arbitrary",),
                                             vmem_limit_bytes=VMEM_LIMIT),
        name="proj",
    )(h, wh, wz, kvg, wuk2, glng, glnb, ws, bs_full)


def _attn_kernel(topk, qabst_ref, qit_ref, wit_ref, ki_ref, c_ref, ct_ref, wuvt_ref, o_ref,
                 st_ref, tau_ref, sca_ref, scb_ref, acc_ref, m_ref, mx_ref):
    i = pl.program_id(1)
    n_keys = (i + 1) * Q_TILE
    n_idx = (n_keys + IDX_CHUNK - 1) // IDX_CHUNK
    n_kv = (n_keys + KV_CHUNK - 1) // KV_CHUNK
    qpos = i * Q_TILE + lax.broadcasted_iota(jnp.int32, (1, Q_TILE), 1)
    kf = float(topk)
    fold_shape = (FOLD_ROWS, Q_TILE)

    def fold_init(v):
        return jnp.full(fold_shape, v, f32)

    def fold_blocks(acc, s, fn):
        for t in range(0, s.shape[0], FOLD_ROWS):
            acc = fn(acc, s[t:t + FOLD_ROWS])
        return acc

    qcat = jnp.concatenate([qit_ref[u, h * IDX_DIM:(h + 1) * IDX_DIM, :]
                            for u in range(N_SUB) for h in range(IDX_HEADS)], axis=1)
    wit = [wit_ref[u] * (IDX_HEADS ** -0.5 * IDX_DIM ** -0.5) for u in range(N_SUB)]

    def scores(j):
        k0 = pl.multiple_of(j * IDX_CHUNK, IDX_CHUNK)
        lg = jnp.dot(ki_ref[pl.ds(k0, IDX_CHUNK), :], qcat, preferred_element_type=f32)
        parts = []
        for u in range(N_SUB):
            s = None
            for h in range(IDX_HEADS):
                col = (u * IDX_HEADS + h) * Q_SUB
                term = wit[u][h:h + 1, :] * jnp.maximum(lg[:, col:col + Q_SUB], 0.0)
                s = term if s is None else s + term
            parts.append(s)
        return k0, jnp.concatenate(parts, axis=1)

    def causal_chunk(j, carry):
        mn, mx = carry
        k0, s = scores(j)
        st_ref[pl.ds(k0, IDX_CHUNK), :] = s
        return fold_blocks(mn, s, jnp.minimum), fold_blocks(mx, s, jnp.maximum)

    def causal_pair(u, carry):
        return causal_chunk(2 * u + 1, causal_chunk(2 * u, carry))

    n_full = n_idx - 1
    mn, mx = lax.fori_loop(0, n_full // 2, causal_pair, (fold_init(jnp.inf), fold_init(-jnp.inf)))
    mn, mx = lax.cond(n_full % 2 == 1, lambda c: causal_chunk(n_full - 1, c), lambda c: c, (mn, mx))
    k0, s = scores(n_idx - 1)
    causal = k0 + lax.broadcasted_iota(jnp.int32, (IDX_CHUNK, 1), 0) <= qpos
    st_ref[pl.ds(k0, IDX_CHUNK), :] = jnp.where(causal, s, -jnp.inf)
    mn = fold_blocks(mn, jnp.where(causal, s, jnp.inf), jnp.minimum)
    mx = fold_blocks(mx, jnp.where(causal, s, -jnp.inf), jnp.maximum)

    def fold_keys(inits, fn, rows):
        full = tuple(jnp.broadcast_to(r, fold_shape) for r in rows)

        def body(j, accs):
            k0 = pl.multiple_of(j * IDX_CHUNK, IDX_CHUNK)
            for t in range(0, IDX_CHUNK, FOLD_ROWS):
                accs = fn(accs, st_ref[pl.ds(k0 + t, FOLD_ROWS), :], *full)
            return accs

        accs = lax.fori_loop(0, n_idx - 1, body, tuple(fold_init(v) for v in inits))
        k0 = pl.multiple_of((n_idx - 1) * IDX_CHUNK, IDX_CHUNK)
        parts = []
        for u in range(N_SUB):
            cols = slice(u * Q_SUB, (u + 1) * Q_SUB)
            part = tuple(a[:, cols] for a in accs)
            sub = tuple(f[:, cols] for f in full)
            for t in range(0, (u + 1) * Q_SUB, FOLD_ROWS):
                part = fn(part, st_ref[pl.ds(k0 + t, FOLD_ROWS), cols], *sub)
            parts.append(part)
        return tuple(jnp.concatenate([p[a] for p in parts], axis=1) for a in range(len(inits)))

    def count_ge(v):
        acc, = fold_keys((0.0,), lambda a, s, vb: (a[0] + jnp.where(s >= vb, 1.0, 0.0),), (v,))
        return jnp.sum(acc, axis=0, keepdims=True)

    @pl.when(i * Q_TILE + Q_TILE <= topk)
    def _():
        tau_ref[...] = jnp.full((1, Q_TILE), -jnp.inf, f32)

    @pl.when(i * Q_TILE + Q_TILE > topk)
    def _():
        lo = jnp.min(mn, axis=0, keepdims=True)
        top = jnp.max(mx, axis=0, keepdims=True)
        hi = top + jnp.maximum(jnp.abs(top) * 1e-3, 1e-30)
        n_lo = (qpos + 1).astype(f32)

        def bisect(_, state):
            lo, hi, n_lo = state
            mid = 0.5 * lo + 0.5 * hi
            n_mid = count_ge(mid)
            ge = n_mid >= kf
            return jnp.where(ge, mid, lo), jnp.where(ge, hi, mid), jnp.where(ge, n_mid, n_lo)

        def count_gt(tau):
            n_gt, = fold_keys((0.0,), lambda g, s, taub: (
                g[0] + jnp.where(s > taub, 1.0, 0.0),), (tau,))
            n_gt = jnp.sum(n_gt, axis=0, keepdims=True)
            return n_gt, jnp.max(jnp.where(n_gt >= kf, 1.0, 0.0))

        lo, hi, n_lo = lax.fori_loop(0, BISECT_STEPS, bisect, (lo, hi, n_lo))
        a, = fold_keys((jnp.inf,), lambda a, s, lob: (
            jnp.minimum(a[0], jnp.where(s >= lob, s, jnp.inf)),), (lo,))
        tau = jnp.min(a, axis=0, keepdims=True)

        def advance(state):
            tau, n_ge, n_gt, _ = state
            nxt, = fold_keys((jnp.inf,), lambda a, s, taub: (
                jnp.minimum(a[0], jnp.where(s > taub, s, jnp.inf)),), (tau,))
            open_row = n_gt >= kf
            tau = jnp.where(open_row, jnp.min(nxt, axis=0, keepdims=True), tau)
            n_ge = jnp.where(open_row, n_gt, n_ge)
            return (tau, n_ge) + count_gt(tau)

        tau, n_ge, n_gt, _ = lax.while_loop(lambda st: st[3] > 0.0, advance,
                                            (tau, n_lo) + count_gt(tau))
        n_eq = n_ge - n_gt
        tau_ref[...] = jnp.where(qpos < topk, -jnp.inf, tau)

        need = kf - n_gt
        surplus = n_eq > need

        @pl.when(jnp.max(jnp.where(surplus, 1.0, 0.0)) > 0.0)
        def _():
            ra = lax.broadcasted_iota(jnp.int32, (IDX_CHUNK, IDX_CHUNK), 0)
            rb = lax.broadcasted_iota(jnp.int32, (IDX_CHUNK, IDX_CHUNK), 1)
            earlier = jnp.where(rb < ra, 1.0, 0.0).astype(bf16)

            def drop(j, seen):
                k0 = pl.multiple_of(j * IDX_CHUNK, IDX_CHUNK)
                s = st_ref[pl.ds(k0, IDX_CHUNK), :]
                eq = jnp.where(s == tau, 1.0, 0.0)
                rank = jnp.dot(earlier, eq.astype(bf16), preferred_element_type=f32) + seen
                st_ref[pl.ds(k0, IDX_CHUNK), :] = jnp.where(
                    s == tau, jnp.where(rank >= need, -jnp.inf, s), s)
                return seen + jnp.sum(eq, axis=0, keepdims=True)

            lax.fori_loop(0, n_idx, drop, jnp.zeros((1, Q_TILE), f32))

    tau = tau_ref[...]
    eye = (lax.broadcasted_iota(jnp.int32, (Q_SUB, Q_SUB), 0)
           == lax.broadcasted_iota(jnp.int32, (Q_SUB, Q_SUB), 1))
    neg_eye = jnp.concatenate([jnp.where(eye, NEG_BIG, 0.0).astype(bf16)] * N_ATT_HEADS, axis=1)
    q_aug = [jnp.concatenate([qabst_ref[u, h * KV_LATENT:(h + 1) * KV_LATENT, :]
                              for h in range(N_ATT_HEADS)], axis=1) for u in range(N_SUB)]
    q_aug = [jnp.concatenate([q, neg_eye], axis=0) for q in q_aug]
    sub_cols = N_ATT_HEADS * Q_SUB

    def stage1(dst, j):
        k0 = pl.multiple_of(j * KV_CHUNK, KV_CHUNK)
        s = st_ref[pl.ds(k0, KV_CHUNK), :]
        kpos = k0 + lax.broadcasted_iota(jnp.int32, (KV_CHUNK, 1), 0)
        dropped = jnp.where(kpos <= qpos, jnp.where(s >= tau, 0.0, 1.0), 1.0).astype(bf16)
        cc = c_ref[pl.ds(k0, KV_CHUNK), :]
        for u in range(N_SUB):
            lhs = jnp.concatenate([cc, dropped[:, u * Q_SUB:(u + 1) * Q_SUB]], axis=1)
            sc = jnp.dot(lhs, q_aug[u], preferred_element_type=f32)
            dst[:, u * sub_cols:(u + 1) * sub_cols] = sc
            mx_ref[:, u * sub_cols:(u + 1) * sub_cols] = jnp.max(sc, axis=0, keepdims=True)

    def stage2(src, j, mx):
        k0 = pl.multiple_of(j * KV_CHUNK, KV_CHUNK)
        m_old = m_ref[...]
        m_new = jnp.maximum(m_old, mx)
        scale = jnp.exp2(m_old - m_new)
        p = jnp.exp2(src[...] - m_new)
        m_ref[...] = m_new
        pv = jnp.dot(ct_ref[:, pl.ds(k0, KV_CHUNK)], p.astype(bf16), preferred_element_type=f32)
        acc_ref[...] = scale * acc_ref[...] + pv

    def stage2_own(src, j, mx):
        k0 = pl.multiple_of(j * KV_CHUNK, KV_CHUNK)
        m_old = m_ref[...]
        m_new = jnp.maximum(m_old, mx)
        scale = jnp.exp2(m_old - m_new)
        m_ref[...] = m_new
        for u in range(N_SUB):
            rows = (u + 1) * Q_SUB
            cols = slice(u * sub_cols, (u + 1) * sub_cols)
            p = jnp.exp2(src[:rows, cols] - m_new[:, cols])
            pv = jnp.dot(ct_ref[:, pl.ds(k0, rows)], p.astype(bf16), preferred_element_type=f32)
            acc_ref[:, cols] = scale[:, cols] * acc_ref[:, cols] + pv

    def step(src, dst, t):
        mx = mx_ref[...]
        stage1(dst, t + 1)
        stage2(src, t, mx)

    m_ref[...] = jnp.full(m_ref.shape, NEG_BIG, f32)
    acc_ref[...] = jnp.zeros(acc_ref.shape, f32)
    stage1(sca_ref, 0)
    n_steps = n_kv - 1

    bufs = (sca_ref, scb_ref)

    def steps(t0, n):
        for r in range(n):
            step(bufs[r % 2], bufs[(r + 1) % 2], t0 + r)

    def unrolled(u, carry):
        steps(STEP_UNROLL * u, STEP_UNROLL)
        return carry

    lax.fori_loop(0, n_steps // STEP_UNROLL, unrolled, 0)
    for rem in range(STEP_UNROLL):
        @pl.when(n_steps % STEP_UNROLL == rem)
        def _(rem=rem):
            steps(n_steps - rem, rem)
            stage2_own(bufs[rem % 2], n_steps, mx_ref[...])

    o_lat = acc_ref[:KV_LATENT, :] / acc_ref[KV_LATENT:KV_LATENT + 1, :]
    for u in range(N_SUB):
        for g in range(N_ATT_HEADS // 2):
            col = u * sub_cols + 2 * g * Q_SUB
            pair = jnp.concatenate([o_lat[:, col:col + Q_SUB], o_lat[:, col + Q_SUB:col + 2 * Q_SUB]],
                                   axis=0).astype(bf16)
            att_t = jnp.dot(wuvt_ref[g], pair, preferred_element_type=f32)
            o_ref[u * Q_SUB:(u + 1) * Q_SUB, g * LANES:(g + 1) * LANES] = att_t.T.astype(bf16)


def _attn(qabst, qit, wit, ki, c, ct, wuvt2, batch, seq, topk):
    T = batch * seq
    nq = seq // Q_TILE

    def qcol(n):
        return pl.BlockSpec((N_SUB, n, Q_SUB), lambda b, i: (b * nq + i, 0, 0))

    return pl.pallas_call(
        functools.partial(_attn_kernel, topk),
        grid=(batch, nq),
        in_specs=[qcol(N_ATT_HEADS * KV_LATENT), qcol(IDX_HEADS * IDX_DIM), qcol(SUBLANES),
                  pl.BlockSpec((seq, IDX_DIM), lambda b, i: (b, 0)),
                  pl.BlockSpec((seq, KV_LATENT), lambda b, i: (b, 0)),
                  pl.BlockSpec((KV_LATENT + SUM_ROWS, seq), lambda b, i: (0, b)),
                  pl.BlockSpec(wuvt2.shape, lambda b, i: (0, 0, 0), pipeline_mode=pl.Buffered(1))],
        out_specs=pl.BlockSpec((Q_TILE, ATT_WIDTH), lambda b, i: (b * nq + i, 0)),
        out_shape=jax.ShapeDtypeStruct((T, ATT_WIDTH), bf16),
        scratch_shapes=[pltpu.VMEM((seq, Q_TILE), f32),
                        pltpu.VMEM((1, Q_TILE), f32),
                        pltpu.VMEM((KV_CHUNK, N_COLS), f32),
                        pltpu.VMEM((KV_CHUNK, N_COLS), f32),
                        pltpu.VMEM((KV_LATENT + SUM_ROWS, N_COLS), f32),
                        pltpu.VMEM((1, N_COLS), f32),
                        pltpu.VMEM((1, N_COLS), f32)],
        compiler_params=pltpu.CompilerParams(dimension_semantics=("arbitrary", "arbitrary"),
                                             vmem_limit_bytes=VMEM_LIMIT),
        name="attn",
    )(qabst, qit, wit, ki, c, ct, wuvt2)


def _out_ffn_ln_kernel(alpha, h_ref, att_ref, gm_ref, wo_ref, g2_ref, b2_ref,
                       w1_ref, w3_ref, w2_ref, g3_ref, b3_ref, o_ref):
    for r0 in range(0, FFN_TILE, FFN_ROWS):
        rows = slice(r0, r0 + FFN_ROWS)
        mix = jnp.concatenate([att_ref[rows, :], gm_ref[rows, :]], axis=1)
        m = jnp.dot(mix, wo_ref[...], preferred_element_type=f32)
        x = _layernorm(alpha * h_ref[rows, :] + m, g2_ref[...], b2_ref[...])
        y = alpha * x + 0.5 * _swiglu(x.astype(bf16), w1_ref, w3_ref, w2_ref)
        o_ref[rows, :] = _layernorm(y, g3_ref[...], b3_ref[...])


def _out_ffn_ln(h, att, gm, wo, g2, b2, w1, w3, w2, g3, b3, alpha):
    T = h.shape[0]
    row = pl.BlockSpec((FFN_TILE, D_MODEL), lambda i: (i, 0))
    half = pl.BlockSpec((FFN_TILE, ATT_WIDTH), lambda i: (i, 0))
    consts = [wo, g2, b2, w1, w3, w2, g3, b3]
    return pl.pallas_call(
        functools.partial(_out_ffn_ln_kernel, alpha),
        grid=(T // FFN_TILE,),
        in_specs=[row, half, half] + [_const_spec(a.shape) for a in consts],
        out_specs=row,
        out_shape=jax.ShapeDtypeStruct((T, D_MODEL), f32),
        compiler_params=pltpu.CompilerParams(dimension_semantics=("arbitrary",),
                                             vmem_limit_bytes=VMEM_LIMIT),
        name="out_ffn_ln",
    )(h, att, gm, *consts)


def _split_w_in(w_in):
    n_head = ATT_WIDTH + KV_LATENT + IDX_HEADS * IDX_DIM + IDX_DIM + IDX_HEADS
    head = jnp.pad(w_in[:, :n_head].astype(bf16), ((0, 0), (0, _PC_Z - n_head)))
    return head, w_in[:, n_head:].astype(bf16)


def _pair_blockdiag(w):
    H, r, c = w.shape
    z = jnp.zeros((H // 2, r, c), w.dtype)
    top = jnp.concatenate([w[0::2], z], axis=2)
    bot = jnp.concatenate([z, w[1::2]], axis=2)
    return jnp.concatenate([top, bot], axis=1)


def kernel(x, ffn1_w1, ffn1_w3, ffn1_w2, ln1_g, ln1_b, w_in, kv_norm_g, w_uk, w_uv, gmlp_ln_g, gmlp_ln_b, gmlp_ws, gmlp_bs, w_out, ln2_g, ln2_b, ffn2_w1, ffn2_w3, ffn2_w2, ln3_g, ln3_b):
    B, L, D = x.shape
    depth = ffn1_w1.shape[0]
    alpha = float((2 * depth) ** 0.25)
    topk = min(TOPK_MAX, L // 4)
    assert D == D_MODEL and L % IDX_CHUNK == 0 and (B * L) % FFN_TILE == 0
    assert KV_CHUNK == Q_TILE and IDX_CHUNK == Q_TILE and L % Q_TILE == 0
    assert max(Q_TILE, KV_CHUNK) % min(Q_TILE, KV_CHUNK) == 0
    assert PROJ_TILE % CHUNK == 0 and IDX_CHUNK % FOLD_ROWS == 0

    def vec(a):
        return a.reshape(1, -1)

    h = x.reshape(B * L, D)
    for l in range(depth):
        h = _ffn_ln(h, ffn1_w1[l].astype(bf16), ffn1_w3[l].astype(bf16), ffn1_w2[l].astype(bf16),
                    vec(ln1_g[l]), vec(ln1_b[l]), alpha)
        wuk2 = _pair_blockdiag(jnp.swapaxes(w_uk[l], 1, 2)).astype(bf16)
        wuvt2 = _pair_blockdiag(jnp.swapaxes(w_uv[l], 1, 2)).astype(bf16)
        bs_full = jnp.repeat(gmlp_bs[l].T, MLP_GROUP_DIM, axis=1)
        qabst, c, ct, qit, ki, wit, gm = _proj(
            h, *_split_w_in(w_in[l]), vec(kv_norm_g[l]), wuk2, vec(gmlp_ln_g[l]), vec(gmlp_ln_b[l]),
            gmlp_ws[l], bs_full)
        att = _attn(qabst, qit, wit, ki, c, ct, wuvt2, B, L, topk)
        h = _out_ffn_ln(h, att, gm, w_out[l].astype(bf16), vec(ln2_g[l]), vec(ln2_b[l]),
                        ffn2_w1[l].astype(bf16), ffn2_w3[l].astype(bf16), ffn2_w2[l].astype(bf16),
                        vec(ln3_g[l]), vec(ln3_b[l]), alpha)
    return h.reshape(B, L, D)
```

```python
import functools

import jax
import jax.numpy as jnp
from jax import lax
from jax.experimental import pallas as pl
from jax.experimental.pallas import tpu as pltpu

D_MODEL = 1024
N_ATT_HEADS = 8
ATT_HEAD_DIM = 64
ATT_WIDTH = N_ATT_HEADS * ATT_HEAD_DIM
KV_LATENT = 128
IDX_HEADS = 4
IDX_DIM = 64
TOPK_MAX = 256
N_MLP_GROUPS = 8
MLP_GROUP_DIM = 64
MLP_WIDTH = N_MLP_GROUPS * MLP_GROUP_DIM
CHUNK = 128
D_FF = 2816
LN_EPS = 1e-5

LANES = 128
SUBLANES = 8
MXU_TILE = 256
Q_SUB = LANES
N_SUB = 4
Q_TILE = N_SUB * Q_SUB
N_COLS = N_ATT_HEADS * Q_TILE
IDX_CHUNK = 512
KV_CHUNK = 512
STEP_UNROLL = 2
FOLD_ROWS = 64
FFN_TILE = 512
FFN_ROWS = 256
FF_CHUNK = D_FF
PROJ_TILE = 512
BISECT_STEPS = 22
VMEM_LIMIT = 56 * 1024 * 1024
NEG_BIG = -(2.0 ** 100)
LOG2E = 1.4426950408889634
SUM_ROWS = 16

_PC_Q = 0
_PC_C = ATT_WIDTH
_PC_QI = _PC_C + KV_LATENT
_PC_KW = _PC_QI + IDX_HEADS * IDX_DIM
_PC_Z = _PC_KW + LANES

bf16 = jnp.bfloat16
f32 = jnp.float32


def _const_spec(shape):
    nd = len(shape)
    return pl.BlockSpec(shape, lambda *_: (0,) * nd, pipeline_mode=pl.Buffered(1))


def _layernorm(y, g, b):
    mu = jnp.mean(y, axis=-1, keepdims=True)
    d = y - mu
    var = jnp.mean(d * d, axis=-1, keepdims=True)
    return d * lax.rsqrt(var + LN_EPS) * g + b


def _swiglu(xb, w1_ref, w3_ref, w2_ref):
    acc = None
    for c0 in range(0, D_FF, FF_CHUNK):
        a = jnp.dot(xb, w1_ref[:, c0:c0 + FF_CHUNK], preferred_element_type=f32)
        b = jnp.dot(xb, w3_ref[:, c0:c0 + FF_CHUNK], preferred_element_type=f32)
        h = (jax.nn.silu(a) * b).astype(bf16)
        part = jnp.dot(h, w2_ref[c0:c0 + FF_CHUNK, :], preferred_element_type=f32)
        acc = part if acc is None else acc + part
    return acc


def _ffn_ln_kernel(alpha, x_ref, w1_ref, w3_ref, w2_ref, g_ref, b_ref, o_ref):
    for r0 in range(0, FFN_TILE, FFN_ROWS):
        x = x_ref[r0:r0 + FFN_ROWS, :]
        y = alpha * x + 0.5 * _swiglu(x.astype(bf16), w1_ref, w3_ref, w2_ref)
        o_ref[r0:r0 + FFN_ROWS, :] = _layernorm(y, g_ref[...], b_ref[...])


def _ffn_ln(x, w1, w3, w2, g, b, alpha):
    T = x.shape[0]
    row = pl.BlockSpec((FFN_TILE, D_MODEL), lambda i: (i, 0))
    return pl.pallas_call(
        functools.partial(_ffn_ln_kernel, alpha),
        grid=(T // FFN_TILE,),
        in_specs=[row, _const_spec(w1.shape), _const_spec(w3.shape), _const_spec(w2.shape),
                  _const_spec(g.shape), _const_spec(b.shape)],
        out_specs=row,
        out_shape=jax.ShapeDtypeStruct((T, D_MODEL), f32),
        compiler_params=pltpu.CompilerParams(dimension_semantics=("arbitrary",),
                                             vmem_limit_bytes=VMEM_LIMIT),
        name="ffn_ln",
    )(x, w1, w3, w2, g, b)


def _proj_kernel(h_ref, wh_ref, wz_ref, kvg_ref, wuk_ref, glng_ref, glnb_ref, ws_ref, bs_ref,
                 qabst_ref, c_ref, ct_ref, qit_ref, ki_ref, wit_ref, gm_ref):
    hb = h_ref[...].astype(bf16)
    proj = jnp.dot(hb, wh_ref[...], preferred_element_type=f32)
    proj_z = jnp.dot(hb, wz_ref[...], preferred_element_type=f32)

    for j in range(N_ATT_HEADS // 2):
        qj = proj[:, _PC_Q + j * LANES:_PC_Q + (j + 1) * LANES].astype(bf16)
        qa = jnp.dot(qj, wuk_ref[j], preferred_element_type=f32) * (ATT_HEAD_DIM ** -0.5 * LOG2E)
        for t in range(PROJ_TILE // Q_SUB):
            qabst_ref[t, 2 * j * KV_LATENT:2 * (j + 1) * KV_LATENT, :] = (
                qa[t * Q_SUB:(t + 1) * Q_SUB, :].T.astype(bf16))

    c = proj[:, _PC_C:_PC_C + KV_LATENT]
    c = c * lax.rsqrt(jnp.mean(c * c, axis=-1, keepdims=True) + LN_EPS) * kvg_ref[...]
    c_ref[...] = c.astype(bf16)
    ones_row = lax.broadcasted_iota(jnp.int32, (SUM_ROWS, PROJ_TILE), 0) == 0
    ct_ref[...] = jnp.concatenate([c.T, jnp.where(ones_row, 1.0, 0.0)], axis=0).astype(bf16)

    qi = proj[:, _PC_QI:_PC_QI + IDX_HEADS * IDX_DIM]
    kw = proj[:, _PC_KW:_PC_KW + LANES]
    ki_ref[...] = kw[:, :IDX_DIM].astype(bf16)
    for t in range(PROJ_TILE // Q_SUB):
        qit_ref[t] = qi[t * Q_SUB:(t + 1) * Q_SUB, :].T.astype(bf16)
        wit_ref[t] = kw[t * Q_SUB:(t + 1) * Q_SUB, :].T[IDX_DIM:IDX_DIM + SUBLANES, :]

    z = jax.nn.gelu(proj_z)
    u = z[:, :MLP_WIDTH]
    v = _layernorm(z[:, MLP_WIDTH:], glng_ref[...], glnb_ref[...]).astype(bf16)
    r = lax.broadcasted_iota(jnp.int32, (CHUNK, CHUNK), 0)
    s = lax.broadcasted_iota(jnp.int32, (CHUNK, CHUNK), 1)
    tri = s <= r
    left = lax.broadcasted_iota(jnp.int32, (CHUNK, LANES), 1) < MLP_GROUP_DIM
    for t0 in range(0, PROJ_TILE, CHUNK):
        for j in range(N_MLP_GROUPS // 2):
            vj = v[t0:t0 + CHUNK, j * LANES:(j + 1) * LANES]
            wa = jnp.where(tri, ws_ref[2 * j], 0.0).astype(bf16)
            wb = jnp.where(tri, ws_ref[2 * j + 1], 0.0).astype(bf16)
            sa = jnp.dot(wa, vj, preferred_element_type=f32)
            sb = jnp.dot(wb, vj, preferred_element_type=f32)
            sv = jnp.where(left, sa, sb) + bs_ref[:, j * LANES:(j + 1) * LANES]
            gm_ref[t0:t0 + CHUNK, j * LANES:(j + 1) * LANES] = (
                u[t0:t0 + CHUNK, j * LANES:(j + 1) * LANES] * sv).astype(bf16)


def _proj(h, wh, wz, kvg, wuk2, glng, glnb, ws, bs_full):
    T = h.shape[0]
    tm = PROJ_TILE

    def rows(n, dt):
        return pl.BlockSpec((tm, n), lambda i: (i, 0)), jax.ShapeDtypeStruct((T, n), dt)

    def cols(n, dt):
        return pl.BlockSpec((n, tm), lambda i: (0, i)), jax.ShapeDtypeStruct((n, T), dt)

    def qtiles(n, dt):
        return (pl.BlockSpec((tm // Q_SUB, n, Q_SUB), lambda i: (i, 0, 0)),
                jax.ShapeDtypeStruct((T // Q_SUB, n, Q_SUB), dt))

    outs = [qtiles(N_ATT_HEADS * KV_LATENT, bf16),
            rows(KV_LATENT, bf16),
            cols(KV_LATENT + SUM_ROWS, bf16),
            qtiles(IDX_HEADS * IDX_DIM, bf16),
            rows(IDX_DIM, bf16),
            qtiles(SUBLANES, f32),
            rows(MLP_WIDTH, bf16)]
    return pl.pallas_call(
        _proj_kernel,
        grid=(T // tm,),
        in_specs=[pl.BlockSpec((tm, D_MODEL), lambda i: (i, 0)),
                  _const_spec(wh.shape), _const_spec(wz.shape), _const_spec(kvg.shape),
                  _const_spec(wuk2.shape),
                  _const_spec(glng.shape), _const_spec(glnb.shape), _const_spec(ws.shape),
                  _const_spec(bs_full.shape)],
        out_specs=[o[0] for o in outs],
        out_shape=[o[1] for o in outs],
        compiler_params=pltpu.CompilerParams(dimension_semantics=("arbitrary",),
                                             vmem_limit_bytes=VMEM_LIMIT),
        name="proj",
    )(h, wh, wz, kvg, wuk2, glng, glnb, ws, bs_full)


def _attn_kernel(topk, qabst_ref, qit_ref, wit_ref, ki_ref, c_ref, ct_ref, wuvt_ref, o_ref,
                 st_ref, tau_ref, sca_ref, scb_ref, acc_ref, m_ref, mx_ref):
    i = pl.program_id(1)
    n_keys = (i + 1) * Q_TILE
    n_idx = (n_keys + IDX_CHUNK - 1) // IDX_CHUNK
    n_kv = (n_keys + KV_CHUNK - 1) // KV_CHUNK
    qpos = i * Q_TILE + lax.broadcasted_iota(jnp.int32, (1, Q_TILE), 1)
    kf = float(topk)
    fold_shape = (FOLD_ROWS, Q_TILE)

    def fold_init(v):
        return jnp.full(fold_shape, v, f32)

    def fold_blocks(acc, s, fn):
        for t in range(0, s.shape[0], FOLD_ROWS):
            acc = fn(acc, s[t:t + FOLD_ROWS])
        return acc

    qcat = jnp.concatenate([qit_ref[u, h * IDX_DIM:(h + 1) * IDX_DIM, :]
                            for u in range(N_SUB) for h in range(IDX_HEADS)], axis=1)
    wit = [wit_ref[u] * (IDX_HEADS ** -0.5 * IDX_DIM ** -0.5) for u in range(N_SUB)]

    def scores(j):
        k0 = pl.multiple_of(j * IDX_CHUNK, IDX_CHUNK)
        lg = jnp.dot(ki_ref[pl.ds(k0, IDX_CHUNK), :], qcat, preferred_element_type=f32)
        parts = []
        for u in range(N_SUB):
            s = None
            for h in range(IDX_HEADS):
                col = (u * IDX_HEADS + h) * Q_SUB
                term = wit[u][h:h + 1, :] * jnp.maximum(lg[:, col:col + Q_SUB], 0.0)
                s = term if s is None else s + term
            parts.append(s)
        return k0, jnp.concatenate(parts, axis=1)

    def causal_chunk(j, carry):
        mn, mx = carry
        k0, s = scores(j)
        st_ref[pl.ds(k0, IDX_CHUNK), :] = s
        return fold_blocks(mn, s, jnp.minimum), fold_blocks(mx, s, jnp.maximum)

    def causal_pair(u, carry):
        return causal_chunk(2 * u + 1, causal_chunk(2 * u, carry))

    n_full = n_idx - 1
    mn, mx = lax.fori_loop(0, n_full // 2, causal_pair, (fold_init(jnp.inf), fold_init(-jnp.inf)))
    mn, mx = lax.cond(n_full % 2 == 1, lambda c: causal_chunk(n_full - 1, c), lambda c: c, (mn, mx))
    k0, s = scores(n_idx - 1)
    causal = k0 + lax.broadcasted_iota(jnp.int32, (IDX_CHUNK, 1), 0) <= qpos
    st_ref[pl.ds(k0, IDX_CHUNK), :] = jnp.where(causal, s, -jnp.inf)
    mn = fold_blocks(mn, jnp.where(causal, s, jnp.inf), jnp.minimum)
    mx = fold_blocks(mx, jnp.where(causal, s, -jnp.inf), jnp.maximum)

    def fold_keys(inits, fn, rows):
        full = tuple(jnp.broadcast_to(r, fold_shape) for r in rows)

        def body(j, accs):
            k0 = pl.multiple_of(j * IDX_CHUNK, IDX_CHUNK)
            for t in range(0, IDX_CHUNK, FOLD_ROWS):
                accs = fn(accs, st_ref[pl.ds(k0 + t, FOLD_ROWS), :], *full)
            return accs

        accs = lax.fori_loop(0, n_idx - 1, body, tuple(fold_init(v) for v in inits))
        k0 = pl.multiple_of((n_idx - 1) * IDX_CHUNK, IDX_CHUNK)
        parts = []
        for u in range(N_SUB):
            cols = slice(u * Q_SUB, (u + 1) * Q_SUB)
            part = tuple(a[:, cols] for a in accs)
            sub = tuple(f[:, cols] for f in full)
            for t in range(0, (u + 1) * Q_SUB, FOLD_ROWS):
                part = fn(part, st_ref[pl.ds(k0 + t, FOLD_ROWS), cols], *sub)
            parts.append(part)
        return tuple(jnp.concatenate([p[a] for p in parts], axis=1) for a in range(len(inits)))

    def count_ge(v):
        acc, = fold_keys((0.0,), lambda a, s, vb: (a[0] + jnp.where(s >= vb, 1.0, 0.0),), (v,))
        return jnp.sum(acc, axis=0, keepdims=True)

    @pl.when(i * Q_TILE + Q_TILE <= topk)
    def _():
        tau_ref[...] = jnp.full((1, Q_TILE), -jnp.inf, f32)

    @pl.when(i * Q_TILE + Q_TILE > topk)
    def _():
        lo = jnp.min(mn, axis=0, keepdims=True)
        top = jnp.max(mx, axis=0, keepdims=True)
        hi = top + jnp.maximum(jnp.abs(top) * 1e-3, 1e-30)
        n_lo = (qpos + 1).astype(f32)

        def bisect(_, state):
            lo, hi, n_lo = state
            mid = 0.5 * lo + 0.5 * hi
            n_mid = count_ge(mid)
            ge = n_mid >= kf
            return jnp.where(ge, mid, lo), jnp.where(ge, hi, mid), jnp.where(ge, n_mid, n_lo)

        def count_gt(tau):
            n_gt, = fold_keys((0.0,), lambda g, s, taub: (
                g[0] + jnp.where(s > taub, 1.0, 0.0),), (tau,))
            n_gt = jnp.sum(n_gt, axis=0, keepdims=True)
            return n_gt, jnp.max(jnp.where(n_gt >= kf, 1.0, 0.0))

        lo, hi, n_lo = lax.fori_loop(0, BISECT_STEPS, bisect, (lo, hi, n_lo))
        a, = fold_keys((jnp.inf,), lambda a, s, lob: (
            jnp.minimum(a[0], jnp.where(s >= lob, s, jnp.inf)),), (lo,))
        tau = jnp.min(a, axis=0, keepdims=True)

        def advance(state):
            tau, n_ge, n_gt, _ = state
            nxt, = fold_keys((jnp.inf,), lambda a, s, taub: (
                jnp.minimum(a[0], jnp.where(s > taub, s, jnp.inf)),), (tau,))
            open_row = n_gt >= kf
            tau = jnp.where(open_row, jnp.min(nxt, axis=0, keepdims=True), tau)
            n_ge = jnp.where(open_row, n_gt, n_ge)
            return (tau, n_ge) + count_gt(tau)

        tau, n_ge, n_gt, _ = lax.while_loop(lambda st: st[3] > 0.0, advance,
                                            (tau, n_lo) + count_gt(tau))
        n_eq = n_ge - n_gt
        tau_ref[...] = jnp.where(qpos < topk, -jnp.inf, tau)

        need = kf - n_gt
        surplus = n_eq > need

        @pl.when(jnp.max(jnp.where(surplus, 1.0, 0.0)) > 0.0)
        def _():
            ra = lax.broadcasted_iota(jnp.int32, (IDX_CHUNK, IDX_CHUNK), 0)
            rb = lax.broadcasted_iota(jnp.int32, (IDX_CHUNK, IDX_CHUNK), 1)
            earlier = jnp.where(rb < ra, 1.0, 0.0).astype(bf16)

            def drop(j, seen):
                k0 = pl.multiple_of(j * IDX_CHUNK, IDX_CHUNK)
                s = st_ref[pl.ds(k0, IDX_CHUNK), :]
                eq = jnp.where(s == tau, 1.0, 0.0)
                rank = jnp.dot(earlier, eq.astype(bf16), preferred_element_type=f32) + seen
                st_ref[pl.ds(k0, IDX_CHUNK), :] = jnp.where(
                    s == tau, jnp.where(rank >= need, -jnp.inf, s), s)
                return seen + jnp.sum(eq, axis=0, keepdims=True)

            lax.fori_loop(0, n_idx, drop, jnp.zeros((1, Q_TILE), f32))

    tau = tau_ref[...]
    eye = (lax.broadcasted_iota(jnp.int32, (Q_SUB, Q_SUB), 0)
           == lax.broadcasted_iota(jnp.int32, (Q_SUB, Q_SUB), 1))
    neg_eye = jnp.concatenate([jnp.where(eye, NEG_BIG, 0.0).astype(bf16)] * N_ATT_HEADS, axis=1)
    q_aug = [jnp.concatenate([qabst_ref[u, h * KV_LATENT:(h + 1) * KV_LATENT, :]
                              for h in range(N_ATT_HEADS)], axis=1) for u in range(N_SUB)]
    q_aug = [jnp.concatenate([q, neg_eye], axis=0) for q in q_aug]
    sub_cols = N_ATT_HEADS * Q_SUB

    def stage1(dst, j):
        k0 = pl.multiple_of(j * KV_CHUNK, KV_CHUNK)
        s = st_ref[pl.ds(k0, KV_CHUNK), :]
        kpos = k0 + lax.broadcasted_iota(jnp.int32, (KV_CHUNK, 1), 0)
        dropped = jnp.where(kpos <= qpos, jnp.where(s >= tau, 0.0, 1.0), 1.0).astype(bf16)
        cc = c_ref[pl.ds(k0, KV_CHUNK), :]
        for u in range(N_SUB):
            lhs = jnp.concatenate([cc, dropped[:, u * Q_SUB:(u + 1) * Q_SUB]], axis=1)
            sc = jnp.dot(lhs, q_aug[u], preferred_element_type=f32)
            dst[:, u * sub_cols:(u + 1) * sub_cols] = sc
            mx_ref[:, u * sub_cols:(u + 1) * sub_cols] = jnp.max(sc, axis=0, keepdims=True)

    def stage2(src, j, mx):
        k0 = pl.multiple_of(j * KV_CHUNK, KV_CHUNK)
        m_old = m_ref[...]
        m_new = jnp.maximum(m_old, mx)
        scale = jnp.exp2(m_old - m_new)
        p = jnp.exp2(src[...] - m_new)
        m_ref[...] = m_new
        pv = jnp.dot(ct_ref[:, pl.ds(k0, KV_CHUNK)], p.astype(bf16), preferred_element_type=f32)
        acc_ref[...] = scale * acc_ref[...] + pv

    def stage2_own(src, j, mx):
        k0 = pl.multiple_of(j * KV_CHUNK, KV_CHUNK)
        m_old = m_ref[...]
        m_new = jnp.maximum(m_old, mx)
        scale = jnp.exp2(m_old - m_new)
        m_ref[...] = m_new
        for u in range(N_SUB):
            rows = (u + 1) * Q_SUB
            cols = slice(u * sub_cols, (u + 1) * sub_cols)
            p = jnp.exp2(src[:rows, cols] - m_new[:, cols])
            pv = jnp.dot(ct_ref[:, pl.ds(k0, rows)], p.astype(bf16), preferred_element_type=f32)
            acc_ref[:, cols] = scale[:, cols] * acc_ref[:, cols] + pv

    def step(src, dst, t):
        mx = mx_ref[...]
        stage1(dst, t + 1)
        stage2(src, t, mx)

    m_ref[...] = jnp.full(m_ref.shape, NEG_BIG, f32)
    acc_ref[...] = jnp.zeros(acc_ref.shape, f32)
    stage1(sca_ref, 0)
    n_steps = n_kv - 1

    bufs = (sca_ref, scb_ref)

    def steps(t0, n):
        for r in range(n):
            step(bufs[r % 2], bufs[(r + 1) % 2], t0 + r)

    def unrolled(u, carry):
        steps(STEP_UNROLL * u, STEP_UNROLL)
        return carry

    lax.fori_loop(0, n_steps // STEP_UNROLL, unrolled, 0)
    for rem in range(STEP_UNROLL):
        @pl.when(n_steps % STEP_UNROLL == rem)
        def _(rem=rem):
            steps(n_steps - rem, rem)
            stage2_own(bufs[rem % 2], n_steps, mx_ref[...])

    o_lat = acc_ref[:KV_LATENT, :] / acc_ref[KV_LATENT:KV_LATENT + 1, :]
    for u in range(N_SUB):
        for g in range(N_ATT_HEADS // 2):
            col = u * sub_cols + 2 * g * Q_SUB
            pair = jnp.concatenate([o_lat[:, col:col + Q_SUB], o_lat[:, col + Q_SUB:col + 2 * Q_SUB]],
                                   axis=0).astype(bf16)
            att_t = jnp.dot(wuvt_ref[g], pair, preferred_element_type=f32)
            o_ref[u * Q_SUB:(u + 1) * Q_SUB, g * LANES:(g + 1) * LANES] = att_t.T.astype(bf16)


def _attn(qabst, qit, wit, ki, c, ct, wuvt2, batch, seq, topk):
    T = batch * seq
    nq = seq // Q_TILE

    def qcol(n):
        return pl.BlockSpec((N_SUB, n, Q_SUB), lambda b, i: (b * nq + i, 0, 0))

    return pl.pallas_call(
        functools.partial(_attn_kernel, topk),
        grid=(batch, nq),
        in_specs=[qcol(N_ATT_HEADS * KV_LATENT), qcol(IDX_HEADS * IDX_DIM), qcol(SUBLANES),
                  pl.BlockSpec((seq, IDX_DIM), lambda b, i: (b, 0)),
                  pl.BlockSpec((seq, KV_LATENT), lambda b, i: (b, 0)),
                  pl.BlockSpec((KV_LATENT + SUM_ROWS, seq), lambda b, i: (0, b)),
                  pl.BlockSpec(wuvt2.shape, lambda b, i: (0, 0, 0), pipeline_mode=pl.Buffered(1))],
        out_specs=pl.BlockSpec((Q_TILE, ATT_WIDTH), lambda b, i: (b * nq + i, 0)),
        out_shape=jax.ShapeDtypeStruct((T, ATT_WIDTH), bf16),
        scratch_shapes=[pltpu.VMEM((seq, Q_TILE), f32),
                        pltpu.VMEM((1, Q_TILE), f32),
                        pltpu.VMEM((KV_CHUNK, N_COLS), f32),
                        pltpu.VMEM((KV_CHUNK, N_COLS), f32),
                        pltpu.VMEM((KV_LATENT + SUM_ROWS, N_COLS), f32),
                        pltpu.VMEM((1, N_COLS), f32),
                        pltpu.VMEM((1, N_COLS), f32)],
        compiler_params=pltpu.CompilerParams(dimension_semantics=("arbitrary", "arbitrary"),
                                             vmem_limit_bytes=VMEM_LIMIT),
        name="attn",
    )(qabst, qit, wit, ki, c, ct, wuvt2)


def _out_ffn_ln_kernel(alpha, h_ref, att_ref, gm_ref, wo_ref, g2_ref, b2_ref,
                       w1_ref, w3_ref, w2_ref, g3_ref, b3_ref, o_ref):
    for r0 in range(0, FFN_TILE, FFN_ROWS):
        rows = slice(r0, r0 + FFN_ROWS)
        mix = jnp.concatenate([att_ref[rows, :], gm_ref[rows, :]], axis=1)
        m = jnp.dot(mix, wo_ref[...], preferred_element_type=f32)
        x = _layernorm(alpha * h_ref[rows, :] + m, g2_ref[...], b2_ref[...])
        y = alpha * x + 0.5 * _swiglu(x.astype(bf16), w1_ref, w3_ref, w2_ref)
        o_ref[rows, :] = _layernorm(y, g3_ref[...], b3_ref[...])


def _out_ffn_ln(h, att, gm, wo, g2, b2, w1, w3, w2, g3, b3, alpha):
    T = h.shape[0]
    row = pl.BlockSpec((FFN_TILE, D_MODEL), lambda i: (i, 0))
    half = pl.BlockSpec((FFN_TILE, ATT_WIDTH), lambda i: (i, 0))
    consts = [wo, g2, b2, w1, w3, w2, g3, b3]
    return pl.pallas_call(
        functools.partial(_out_ffn_ln_kernel, alpha),
        grid=(T // FFN_TILE,),
        in_specs=[row, half, half] + [_const_spec(a.shape) for a in consts],
        out_specs=row,
        out_shape=jax.ShapeDtypeStruct((T, D_MODEL), f32),
        compiler_params=pltpu.CompilerParams(dimension_semantics=("arbitrary",),
                                             vmem_limit_bytes=VMEM_LIMIT),
        name="out_ffn_ln",
    )(h, att, gm, *consts)


def _split_w_in(w_in):
    n_head = ATT_WIDTH + KV_LATENT + IDX_HEADS * IDX_DIM + IDX_DIM + IDX_HEADS
    head = jnp.pad(w_in[:, :n_head].astype(bf16), ((0, 0), (0, _PC_Z - n_head)))
    return head, w_in[:, n_head:].astype(bf16)


def _pair_blockdiag(w):
    H, r, c = w.shape
    z = jnp.zeros((H // 2, r, c), w.dtype)
    top = jnp.concatenate([w[0::2], z], axis=2)
    bot = jnp.concatenate([z, w[1::2]], axis=2)
    return jnp.concatenate([top, bot], axis=1)


def kernel(x, ffn1_w1, ffn1_w3, ffn1_w2, ln1_g, ln1_b, w_in, kv_norm_g, w_uk, w_uv, gmlp_ln_g, gmlp_ln_b, gmlp_ws, gmlp_bs, w_out, ln2_g, ln2_b, ffn2_w1, ffn2_w3, ffn2_w2, ln3_g, ln3_b):
    B, L, D = x.shape
    depth = ffn1_w1.shape[0]
    alpha = float((2 * depth) ** 0.25)
    topk = min(TOPK_MAX, L // 4)
    assert D == D_MODEL and L % IDX_CHUNK == 0 and (B * L) % FFN_TILE == 0
    assert KV_CHUNK == Q_TILE and IDX_CHUNK == Q_TILE and L % Q_TILE == 0
    assert D_FF % FF_CHUNK == 0 and FF_CHUNK % MXU_TILE == 0 and STEP_UNROLL % 2 == 0
    assert FFN_TILE % FFN_ROWS == 0 and (B * L) % PROJ_TILE == 0 and L % PROJ_TILE == 0
    assert PROJ_TILE % CHUNK == 0 and IDX_CHUNK % FOLD_ROWS == 0 and Q_SUB % FOLD_ROWS == 0

    def vec(a):
        return a.reshape(1, -1)

    h = x.reshape(B * L, D)
    for l in range(depth):
        h = _ffn_ln(h, ffn1_w1[l].astype(bf16), ffn1_w3[l].astype(bf16), ffn1_w2[l].astype(bf16),
                    vec(ln1_g[l]), vec(ln1_b[l]), alpha)
        wuk2 = _pair_blockdiag(jnp.swapaxes(w_uk[l], 1, 2)).astype(bf16)
        wuvt2 = _pair_blockdiag(jnp.swapaxes(w_uv[l], 1, 2)).astype(bf16)
        bs_full = jnp.repeat(gmlp_bs[l].T, MLP_GROUP_DIM, axis=1)
        qabst, c, ct, qit, ki, wit, gm = _proj(
            h, *_split_w_in(w_in[l]), vec(kv_norm_g[l]), wuk2, vec(gmlp_ln_g[l]), vec(gmlp_ln_b[l]),
            gmlp_ws[l], bs_full)
        att = _attn(qabst, qit, wit, ki, c, ct, wuvt2, B, L, topk)
        h = _out_ffn_ln(h, att, gm, w_out[l].astype(bf16), vec(ln2_g[l]), vec(ln2_b[l]),
                        ffn2_w1[l].astype(bf16), ffn2_w3[l].astype(bf16), ffn2_w2[l].astype(bf16),
                        vec(ln3_g[l]), vec(ln3_b[l]), alpha)
    return h.reshape(B, L, D)
```

```python
import functools

import jax
import jax.numpy as jnp
from jax import lax
from jax.experimental import pallas as pl
from jax.experimental.pallas import tpu as pltpu

D_MODEL = 1024
N_ATT_HEADS = 8
ATT_HEAD_DIM = 64
ATT_WIDTH = N_ATT_HEADS * ATT_HEAD_DIM
KV_LATENT = 128
IDX_HEADS = 4
IDX_DIM = 64
TOPK_MAX = 256
N_MLP_GROUPS = 8
MLP_GROUP_DIM = 64
MLP_WIDTH = N_MLP_GROUPS * MLP_GROUP_DIM
CHUNK = 128
D_FF = 2816
LN_EPS = 1e-5

LANES = 128
SUBLANES = 8
MXU_TILE = 256
Q_SUB = LANES
N_SUB = 4
Q_TILE = N_SUB * Q_SUB
N_COLS = N_ATT_HEADS * Q_TILE
IDX_CHUNK = 512
KV_CHUNK = 512
STEP_UNROLL = 2
FOLD_ROWS = 64
FFN_TILE = 512
FFN_ROWS = 256
FF_CHUNK = D_FF
PROJ_TILE = 512
BISECT_STEPS = 19
PROBE_SPREAD = 0.35
VMEM_LIMIT = 56 * 1024 * 1024
NEG_BIG = -(2.0 ** 100)
LOG2E = 1.4426950408889634
SUM_ROWS = 16

_PC_Q = 0
_PC_C = ATT_WIDTH
_PC_QI = _PC_C + KV_LATENT
_PC_KW = _PC_QI + IDX_HEADS * IDX_DIM
_PC_Z = _PC_KW + LANES

bf16 = jnp.bfloat16
f32 = jnp.float32


def _const_spec(shape):
    nd = len(shape)
    return pl.BlockSpec(shape, lambda *_: (0,) * nd, pipeline_mode=pl.Buffered(1))


def _layernorm(y, g, b):
    mu = jnp.mean(y, axis=-1, keepdims=True)
    d = y - mu
    var = jnp.mean(d * d, axis=-1, keepdims=True)
    return d * lax.rsqrt(var + LN_EPS) * g + b


def _swiglu(xb, w1_ref, w3_ref, w2_ref):
    acc = None
    for c0 in range(0, D_FF, FF_CHUNK):
        a = jnp.dot(xb, w1_ref[:, c0:c0 + FF_CHUNK], preferred_element_type=f32)
        b = jnp.dot(xb, w3_ref[:, c0:c0 + FF_CHUNK], preferred_element_type=f32)
        h = (jax.nn.silu(a) * b).astype(bf16)
        part = jnp.dot(h, w2_ref[c0:c0 + FF_CHUNK, :], preferred_element_type=f32)
        acc = part if acc is None else acc + part
    return acc


def _ffn_ln_kernel(alpha, x_ref, w1_ref, w3_ref, w2_ref, g_ref, b_ref, o_ref):
    for r0 in range(0, FFN_TILE, FFN_ROWS):
        x = x_ref[r0:r0 + FFN_ROWS, :]
        y = alpha * x + 0.5 * _swiglu(x.astype(bf16), w1_ref, w3_ref, w2_ref)
        o_ref[r0:r0 + FFN_ROWS, :] = _layernorm(y, g_ref[...], b_ref[...])


def _ffn_ln(x, w1, w3, w2, g, b, alpha):
    T = x.shape[0]
    row = pl.BlockSpec((FFN_TILE, D_MODEL), lambda i: (i, 0))
    return pl.pallas_call(
        functools.partial(_ffn_ln_kernel, alpha),
        grid=(T // FFN_TILE,),
        in_specs=[row, _const_spec(w1.shape), _const_spec(w3.shape), _const_spec(w2.shape),
                  _const_spec(g.shape), _const_spec(b.shape)],
        out_specs=row,
        out_shape=jax.ShapeDtypeStruct((T, D_MODEL), f32),
        compiler_params=pltpu.CompilerParams(dimension_semantics=("arbitrary",),
                                             vmem_limit_bytes=VMEM_LIMIT),
        name="ffn_ln",
    )(x, w1, w3, w2, g, b)


def _proj_kernel(h_ref, wh_ref, wz_ref, kvg_ref, wuk_ref, glng_ref, glnb_ref, ws_ref, bs_ref,
                 qabst_ref, c_ref, ct_ref, qit_ref, ki_ref, wit_ref, gm_ref):
    hb = h_ref[...].astype(bf16)
    proj = jnp.dot(hb, wh_ref[...], preferred_element_type=f32)
    proj_z = jnp.dot(hb, wz_ref[...], preferred_element_type=f32)

    for j in range(N_ATT_HEADS // 2):
        qj = proj[:, _PC_Q + j * LANES:_PC_Q + (j + 1) * LANES].astype(bf16)
        qa = jnp.dot(qj, wuk_ref[j], preferred_element_type=f32) * (ATT_HEAD_DIM ** -0.5 * LOG2E)
        for t in range(PROJ_TILE // Q_SUB):
            qabst_ref[t, 2 * j * KV_LATENT:2 * (j + 1) * KV_LATENT, :] = (
                qa[t * Q_SUB:(t + 1) * Q_SUB, :].T.astype(bf16))

    c = proj[:, _PC_C:_PC_C + KV_LATENT]
    c = c * lax.rsqrt(jnp.mean(c * c, axis=-1, keepdims=True) + LN_EPS) * kvg_ref[...]
    c_ref[...] = c.astype(bf16)
    ones_row = lax.broadcasted_iota(jnp.int32, (SUM_ROWS, PROJ_TILE), 0) == 0
    ct_ref[...] = jnp.concatenate([c.T, jnp.where(ones_row, 1.0, 0.0)], axis=0).astype(bf16)

    qi = proj[:, _PC_QI:_PC_QI + IDX_HEADS * IDX_DIM]
    kw = proj[:, _PC_KW:_PC_KW + LANES]
    ki_ref[...] = kw[:, :IDX_DIM].astype(bf16)
    for t in range(PROJ_TILE // Q_SUB):
        qit_ref[t] = qi[t * Q_SUB:(t + 1) * Q_SUB, :].T.astype(bf16)
        wit_ref[t] = kw[t * Q_SUB:(t + 1) * Q_SUB, :].T[IDX_DIM:IDX_DIM + SUBLANES, :]

    z = jax.nn.gelu(proj_z)
    u = z[:, :MLP_WIDTH]
    v = _layernorm(z[:, MLP_WIDTH:], glng_ref[...], glnb_ref[...]).astype(bf16)
    r = lax.broadcasted_iota(jnp.int32, (CHUNK, CHUNK), 0)
    s = lax.broadcasted_iota(jnp.int32, (CHUNK, CHUNK), 1)
    tri = s <= r
    left = lax.broadcasted_iota(jnp.int32, (CHUNK, LANES), 1) < MLP_GROUP_DIM
    for t0 in range(0, PROJ_TILE, CHUNK):
        for j in range(N_MLP_GROUPS // 2):
            vj = v[t0:t0 + CHUNK, j * LANES:(j + 1) * LANES]
            wa = jnp.where(tri, ws_ref[2 * j], 0.0).astype(bf16)
            wb = jnp.where(tri, ws_ref[2 * j + 1], 0.0).astype(bf16)
            sa = jnp.dot(wa, vj, preferred_element_type=f32)
            sb = jnp.dot(wb, vj, preferred_element_type=f32)
            sv = jnp.where(left, sa, sb) + bs_ref[:, j * LANES:(j + 1) * LANES]
            gm_ref[t0:t0 + CHUNK, j * LANES:(j + 1) * LANES] = (
                u[t0:t0 + CHUNK, j * LANES:(j + 1) * LANES] * sv).astype(bf16)


def _proj(h, wh, wz, kvg, wuk2, glng, glnb, ws, bs_full):
    T = h.shape[0]
    tm = PROJ_TILE

    def rows(n, dt):
        return pl.BlockSpec((tm, n), lambda i: (i, 0)), jax.ShapeDtypeStruct((T, n), dt)

    def cols(n, dt):
        return pl.BlockSpec((n, tm), lambda i: (0, i)), jax.ShapeDtypeStruct((n, T), dt)

    def qtiles(n, dt):
        return (pl.BlockSpec((tm // Q_SUB, n, Q_SUB), lambda i: (i, 0, 0)),
                jax.ShapeDtypeStruct((T // Q_SUB, n, Q_SUB), dt))

    outs = [qtiles(N_ATT_HEADS * KV_LATENT, bf16),
            rows(KV_LATENT, bf16),
            cols(KV_LATENT + SUM_ROWS, bf16),
            qtiles(IDX_HEADS * IDX_DIM, bf16),
            rows(IDX_DIM, bf16),
            qtiles(SUBLANES, f32),
            rows(MLP_WIDTH, bf16)]
    return pl.pallas_call(
        _proj_kernel,
        grid=(T // tm,),
        in_specs=[pl.BlockSpec((tm, D_MODEL), lambda i: (i, 0)),
                  _const_spec(wh.shape), _const_spec(wz.shape), _const_spec(kvg.shape),
                  _const_spec(wuk2.shape),
                  _const_spec(glng.shape), _const_spec(glnb.shape), _const_spec(ws.shape),
                  _const_spec(bs_full.shape)],
        out_specs=[o[0] for o in outs],
        out_shape=[o[1] for o in outs],
        compiler_params=pltpu.CompilerParams(dimension_semantics=("arbitrary",),
                                             vmem_limit_bytes=VMEM_LIMIT),
        name="proj",
    )(h, wh, wz, kvg, wuk2, glng, glnb, ws, bs_full)


def _attn_kernel(topk, qabst_ref, qit_ref, wit_ref, ki_ref, c_ref, ct_ref, wuvt_ref, o_ref,
                 st_ref, tau_ref, sca_ref, scb_ref, acc_ref, m_ref, mx_ref):
    i = pl.program_id(1)
    n_keys = (i + 1) * Q_TILE
    n_idx = (n_keys + IDX_CHUNK - 1) // IDX_CHUNK
    n_kv = (n_keys + KV_CHUNK - 1) // KV_CHUNK
    qpos = i * Q_TILE + lax.broadcasted_iota(jnp.int32, (1, Q_TILE), 1)
    kf = float(topk)
    fold_shape = (FOLD_ROWS, Q_TILE)

    def fold_init(v):
        return jnp.full(fold_shape, v, f32)

    def fold_blocks(acc, s, fn):
        for t in range(0, s.shape[0], FOLD_ROWS):
            acc = fn(acc, s[t:t + FOLD_ROWS])
        return acc

    qcat = jnp.concatenate([qit_ref[u, h * IDX_DIM:(h + 1) * IDX_DIM, :]
                            for u in range(N_SUB) for h in range(IDX_HEADS)], axis=1)
    wit = [wit_ref[u] * (IDX_HEADS ** -0.5 * IDX_DIM ** -0.5) for u in range(N_SUB)]

    def scores(j):
        k0 = pl.multiple_of(j * IDX_CHUNK, IDX_CHUNK)
        lg = jnp.dot(ki_ref[pl.ds(k0, IDX_CHUNK), :], qcat, preferred_element_type=f32)
        parts = []
        for u in range(N_SUB):
            s = None
            for h in range(IDX_HEADS):
                col = (u * IDX_HEADS + h) * Q_SUB
                term = wit[u][h:h + 1, :] * jnp.maximum(lg[:, col:col + Q_SUB], 0.0)
                s = term if s is None else s + term
            parts.append(s)
        return k0, jnp.concatenate(parts, axis=1)

    def moments(carry, s_min, s_max, s_zero):
        mn, mx, s1, s2 = carry
        return (fold_blocks(mn, s_min, jnp.minimum), fold_blocks(mx, s_max, jnp.maximum),
                fold_blocks(s1, s_zero, jnp.add), fold_blocks(s2, s_zero * s_zero, jnp.add))

    def causal_chunk(j, carry):
        k0, s = scores(j)
        st_ref[pl.ds(k0, IDX_CHUNK), :] = s
        return moments(carry, s, s, s)

    def causal_pair(u, carry):
        return causal_chunk(2 * u + 1, causal_chunk(2 * u, carry))

    n_full = n_idx - 1
    stats = (fold_init(jnp.inf), fold_init(-jnp.inf), fold_init(0.0), fold_init(0.0))
    stats = lax.fori_loop(0, n_full // 2, causal_pair, stats)
    stats = lax.cond(n_full % 2 == 1, lambda c: causal_chunk(n_full - 1, c), lambda c: c, stats)
    k0, s = scores(n_idx - 1)
    causal = k0 + lax.broadcasted_iota(jnp.int32, (IDX_CHUNK, 1), 0) <= qpos
    st_ref[pl.ds(k0, IDX_CHUNK), :] = jnp.where(causal, s, -jnp.inf)
    mn, mx, s1, s2 = moments(stats, jnp.where(causal, s, jnp.inf), jnp.where(causal, s, -jnp.inf),
                             jnp.where(causal, s, 0.0))

    def fold_keys(inits, fn, rows):
        full = tuple(jnp.broadcast_to(r, fold_shape) for r in rows)

        def body(j, accs):
            k0 = pl.multiple_of(j * IDX_CHUNK, IDX_CHUNK)
            for t in range(0, IDX_CHUNK, FOLD_ROWS):
                accs = fn(accs, st_ref[pl.ds(k0 + t, FOLD_ROWS), :], *full)
            return accs

        accs = lax.fori_loop(0, n_idx - 1, body, tuple(fold_init(v) for v in inits))
        k0 = pl.multiple_of((n_idx - 1) * IDX_CHUNK, IDX_CHUNK)
        parts = []
        for u in range(N_SUB):
            cols = slice(u * Q_SUB, (u + 1) * Q_SUB)
            part = tuple(a[:, cols] for a in accs)
            sub = tuple(f[:, cols] for f in full)
            for t in range(0, (u + 1) * Q_SUB, FOLD_ROWS):
                part = fn(part, st_ref[pl.ds(k0 + t, FOLD_ROWS), cols], *sub)
            parts.append(part)
        return tuple(jnp.concatenate([p[a] for p in parts], axis=1) for a in range(len(inits)))

    def count_ge(v):
        acc, = fold_keys((0.0,), lambda a, s, vb: (a[0] + jnp.where(s >= vb, 1.0, 0.0),), (v,))
        return jnp.sum(acc, axis=0, keepdims=True)

    @pl.when(i * Q_TILE + Q_TILE <= topk)
    def _():
        tau_ref[...] = jnp.full((1, Q_TILE), -jnp.inf, f32)

    @pl.when(i * Q_TILE + Q_TILE > topk)
    def _():
        lo = jnp.min(mn, axis=0, keepdims=True)
        top = jnp.max(mx, axis=0, keepdims=True)
        hi = top + jnp.maximum(jnp.abs(top) * 1e-3, 1e-30)
        n_lo = (qpos + 1).astype(f32)

        def bisect(_, state):
            lo, hi, n_lo = state
            mid = 0.5 * lo + 0.5 * hi
            n_mid = count_ge(mid)
            ge = n_mid >= kf
            return jnp.where(ge, mid, lo), jnp.where(ge, hi, mid), jnp.where(ge, n_mid, n_lo)

        def count_gt(tau):
            n_gt, = fold_keys((0.0,), lambda g, s, taub: (
                g[0] + jnp.where(s > taub, 1.0, 0.0),), (tau,))
            n_gt = jnp.sum(n_gt, axis=0, keepdims=True)
            return n_gt, jnp.max(jnp.where(n_gt >= kf, 1.0, 0.0))

        n_vis = (qpos + 1).astype(f32)
        mu = jnp.sum(s1, axis=0, keepdims=True) / n_vis
        sd = jnp.sqrt(jnp.maximum(jnp.sum(s2, axis=0, keepdims=True) / n_vis - mu * mu, 0.0))
        p = jnp.clip(kf / n_vis, 1e-6, 1.0 - 1e-6)
        t = jnp.sqrt(-2.0 * jnp.log(jnp.minimum(p, 1.0 - p)))
        z = t - ((0.010328 * t + 0.802853) * t + 2.515517) / (((0.001308 * t + 0.189269) * t + 1.432788) * t + 1.0)
        z = jnp.where(p <= 0.5, z, -z)
        for dz in (-PROBE_SPREAD, PROBE_SPREAD):
            v = mu + (z + dz) * sd
            n_v = count_ge(v)
            up = (n_v >= kf) & (v > lo)
            hi = jnp.where((n_v < kf) & (v < hi), v, hi)
            lo = jnp.where(up, v, lo)
            n_lo = jnp.where(up, n_v, n_lo)

        lo, hi, n_lo = lax.fori_loop(0, BISECT_STEPS, bisect, (lo, hi, n_lo))
        a, = fold_keys((jnp.inf,), lambda a, s, lob: (
            jnp.minimum(a[0], jnp.where(s >= lob, s, jnp.inf)),), (lo,))
        tau = jnp.min(a, axis=0, keepdims=True)

        def advance(state):
            tau, n_ge, n_gt, _ = state
            nxt, = fold_keys((jnp.inf,), lambda a, s, taub: (
                jnp.minimum(a[0], jnp.where(s > taub, s, jnp.inf)),), (tau,))
            open_row = n_gt >= kf
            tau = jnp.where(open_row, jnp.min(nxt, axis=0, keepdims=True), tau)
            n_ge = jnp.where(open_row, n_gt, n_ge)
            return (tau, n_ge) + count_gt(tau)

        tau, n_ge, n_gt, _ = lax.while_loop(lambda st: st[3] > 0.0, advance,
                                            (tau, n_lo) + count_gt(tau))
        n_eq = n_ge - n_gt
        tau_ref[...] = jnp.where(qpos < topk, -jnp.inf, tau)

        need = kf - n_gt
        surplus = n_eq > need

        @pl.when(jnp.max(jnp.where(surplus, 1.0, 0.0)) > 0.0)
        def _():
            ra = lax.broadcasted_iota(jnp.int32, (IDX_CHUNK, IDX_CHUNK), 0)
            rb = lax.broadcasted_iota(jnp.int32, (IDX_CHUNK, IDX_CHUNK), 1)
            earlier = jnp.where(rb < ra, 1.0, 0.0).astype(bf16)

            def drop(j, seen):
                k0 = pl.multiple_of(j * IDX_CHUNK, IDX_CHUNK)
                s = st_ref[pl.ds(k0, IDX_CHUNK), :]
                eq = jnp.where(s == tau, 1.0, 0.0)
                rank = jnp.dot(earlier, eq.astype(bf16), preferred_element_type=f32) + seen
                st_ref[pl.ds(k0, IDX_CHUNK), :] = jnp.where(
                    s == tau, jnp.where(rank >= need, -jnp.inf, s), s)
                return seen + jnp.sum(eq, axis=0, keepdims=True)

            lax.fori_loop(0, n_idx, drop, jnp.zeros((1, Q_TILE), f32))

    tau = tau_ref[...]
    eye = (lax.broadcasted_iota(jnp.int32, (Q_SUB, Q_SUB), 0)
           == lax.broadcasted_iota(jnp.int32, (Q_SUB, Q_SUB), 1))
    neg_eye = jnp.concatenate([jnp.where(eye, NEG_BIG, 0.0).astype(bf16)] * N_ATT_HEADS, axis=1)
    q_aug = [jnp.concatenate([qabst_ref[u, h * KV_LATENT:(h + 1) * KV_LATENT, :]
                              for h in range(N_ATT_HEADS)], axis=1) for u in range(N_SUB)]
    q_aug = [jnp.concatenate([q, neg_eye], axis=0) for q in q_aug]
    sub_cols = N_ATT_HEADS * Q_SUB

    def stage1(dst, j):
        k0 = pl.multiple_of(j * KV_CHUNK, KV_CHUNK)
        s = st_ref[pl.ds(k0, KV_CHUNK), :]
        kpos = k0 + lax.broadcasted_iota(jnp.int32, (KV_CHUNK, 1), 0)
        dropped = jnp.where(kpos <= qpos, jnp.where(s >= tau, 0.0, 1.0), 1.0).astype(bf16)
        cc = c_ref[pl.ds(k0, KV_CHUNK), :]
        for u in range(N_SUB):
            lhs = jnp.concatenate([cc, dropped[:, u * Q_SUB:(u + 1) * Q_SUB]], axis=1)
            sc = jnp.dot(lhs, q_aug[u], preferred_element_type=f32)
            dst[:, u * sub_cols:(u + 1) * sub_cols] = sc
            mx_ref[:, u * sub_cols:(u + 1) * sub_cols] = jnp.max(sc, axis=0, keepdims=True)

    def stage2(src, j, mx):
        k0 = pl.multiple_of(j * KV_CHUNK, KV_CHUNK)
        m_old = m_ref[...]
        m_new = jnp.maximum(m_old, mx)
        scale = jnp.exp2(m_old - m_new)
        p = jnp.exp2(src[...] - m_new)
        m_ref[...] = m_new
        pv = jnp.dot(ct_ref[:, pl.ds(k0, KV_CHUNK)], p.astype(bf16), preferred_element_type=f32)
        acc_ref[...] = scale * acc_ref[...] + pv

    def stage2_own(src, j, mx):
        k0 = pl.multiple_of(j * KV_CHUNK, KV_CHUNK)
        m_old = m_ref[...]
        m_new = jnp.maximum(m_old, mx)
        scale = jnp.exp2(m_old - m_new)
        m_ref[...] = m_new
        for u in range(N_SUB):
            rows = (u + 1) * Q_SUB
            cols = slice(u * sub_cols, (u + 1) * sub_cols)
            p = jnp.exp2(src[:rows, cols] - m_new[:, cols])
            pv = jnp.dot(ct_ref[:, pl.ds(k0, rows)], p.astype(bf16), preferred_element_type=f32)
            acc_ref[:, cols] = scale[:, cols] * acc_ref[:, cols] + pv

    def step(src, dst, t):
        mx = mx_ref[...]
        stage1(dst, t + 1)
        stage2(src, t, mx)

    m_ref[...] = jnp.full(m_ref.shape, NEG_BIG, f32)
    acc_ref[...] = jnp.zeros(acc_ref.shape, f32)
    stage1(sca_ref, 0)
    n_steps = n_kv - 1

    bufs = (sca_ref, scb_ref)

    def steps(t0, n):
        for r in range(n):
            step(bufs[r % 2], bufs[(r + 1) % 2], t0 + r)

    def unrolled(u, carry):
        steps(STEP_UNROLL * u, STEP_UNROLL)
        return carry

    lax.fori_loop(0, n_steps // STEP_UNROLL, unrolled, 0)
    for rem in range(STEP_UNROLL):
        @pl.when(n_steps % STEP_UNROLL == rem)
        def _(rem=rem):
            steps(n_steps - rem, rem)
            stage2_own(bufs[rem % 2], n_steps, mx_ref[...])

    o_lat = acc_ref[:KV_LATENT, :] / acc_ref[KV_LATENT:KV_LATENT + 1, :]
    for u in range(N_SUB):
        for g in range(N_ATT_HEADS // 2):
            col = u * sub_cols + 2 * g * Q_SUB
            pair = jnp.concatenate([o_lat[:, col:col + Q_SUB], o_lat[:, col + Q_SUB:col + 2 * Q_SUB]],
                                   axis=0).astype(bf16)
            att_t = jnp.dot(wuvt_ref[g], pair, preferred_element_type=f32)
            o_ref[u * Q_SUB:(u + 1) * Q_SUB, g * LANES:(g + 1) * LANES] = att_t.T.astype(bf16)


def _attn(qabst, qit, wit, ki, c, ct, wuvt2, batch, seq, topk):
    T = batch * seq
    nq = seq // Q_TILE

    def qcol(n):
        return pl.BlockSpec((N_SUB, n, Q_SUB), lambda b, i: (b * nq + i, 0, 0))

    return pl.pallas_call(
        functools.partial(_attn_kernel, topk),
        grid=(batch, nq),
        in_specs=[qcol(N_ATT_HEADS * KV_LATENT), qcol(IDX_HEADS * IDX_DIM), qcol(SUBLANES),
                  pl.BlockSpec((seq, IDX_DIM), lambda b, i: (b, 0)),
                  pl.BlockSpec((seq, KV_LATENT), lambda b, i: (b, 0)),
                  pl.BlockSpec((KV_LATENT + SUM_ROWS, seq), lambda b, i: (0, b)),
                  pl.BlockSpec(wuvt2.shape, lambda b, i: (0, 0, 0), pipeline_mode=pl.Buffered(1))],
        out_specs=pl.BlockSpec((Q_TILE, ATT_WIDTH), lambda b, i: (b * nq + i, 0)),
        out_shape=jax.ShapeDtypeStruct((T, ATT_WIDTH), bf16),
        scratch_shapes=[pltpu.VMEM((seq, Q_TILE), f32),
                        pltpu.VMEM((1, Q_TILE), f32),
                        pltpu.VMEM((KV_CHUNK, N_COLS), f32),
                        pltpu.VMEM((KV_CHUNK, N_COLS), f32),
                        pltpu.VMEM((KV_LATENT + SUM_ROWS, N_COLS), f32),
                        pltpu.VMEM((1, N_COLS), f32),
                        pltpu.VMEM((1, N_COLS), f32)],
        compiler_params=pltpu.CompilerParams(dimension_semantics=("arbitrary", "arbitrary"),
                                             vmem_limit_bytes=VMEM_LIMIT),
        name="attn",
    )(qabst, qit, wit, ki, c, ct, wuvt2)


def _out_ffn_ln_kernel(alpha, h_ref, att_ref, gm_ref, wo_ref, g2_ref, b2_ref,
                       w1_ref, w3_ref, w2_ref, g3_ref, b3_ref, o_ref):
    for r0 in range(0, FFN_TILE, FFN_ROWS):
        rows = slice(r0, r0 + FFN_ROWS)
        mix = jnp.concatenate([att_ref[rows, :], gm_ref[rows, :]], axis=1)
        m = jnp.dot(mix, wo_ref[...], preferred_element_type=f32)
        x = _layernorm(alpha * h_ref[rows, :] + m, g2_ref[...], b2_ref[...])
        y = alpha * x + 0.5 * _swiglu(x.astype(bf16), w1_ref, w3_ref, w2_ref)
        o_ref[rows, :] = _layernorm(y, g3_ref[...], b3_ref[...])


def _out_ffn_ln(h, att, gm, wo, g2, b2, w1, w3, w2, g3, b3, alpha):
    T = h.shape[0]
    row = pl.BlockSpec((FFN_TILE, D_MODEL), lambda i: (i, 0))
    half = pl.BlockSpec((FFN_TILE, ATT_WIDTH), lambda i: (i, 0))
    consts = [wo, g2, b2, w1, w3, w2, g3, b3]
    return pl.pallas_call(
        functools.partial(_out_ffn_ln_kernel, alpha),
        grid=(T // FFN_TILE,),
        in_specs=[row, half, half] + [_const_spec(a.shape) for a in consts],
        out_specs=row,
        out_shape=jax.ShapeDtypeStruct((T, D_MODEL), f32),
        compiler_params=pltpu.CompilerParams(dimension_semantics=("arbitrary",),
                                             vmem_limit_bytes=VMEM_LIMIT),
        name="out_ffn_ln",
    )(h, att, gm, *consts)


def _split_w_in(w_in):
    n_head = ATT_WIDTH + KV_LATENT + IDX_HEADS * IDX_DIM + IDX_DIM + IDX_HEADS
    head = jnp.pad(w_in[:, :n_head].astype(bf16), ((0, 0), (0, _PC_Z - n_head)))
    return head, w_in[:, n_head:].astype(bf16)


def _pair_blockdiag(w):
    H, r, c = w.shape
    z = jnp.zeros((H // 2, r, c), w.dtype)
    top = jnp.concatenate([w[0::2], z], axis=2)
    bot = jnp.concatenate([z, w[1::2]], axis=2)
    return jnp.concatenate([top, bot], axis=1)


def kernel(x, ffn1_w1, ffn1_w3, ffn1_w2, ln1_g, ln1_b, w_in, kv_norm_g, w_uk, w_uv, gmlp_ln_g, gmlp_ln_b, gmlp_ws, gmlp_bs, w_out, ln2_g, ln2_b, ffn2_w1, ffn2_w3, ffn2_w2, ln3_g, ln3_b):
    B, L, D = x.shape
    depth = ffn1_w1.shape[0]
    alpha = float((2 * depth) ** 0.25)
    topk = min(TOPK_MAX, L // 4)
    assert D == D_MODEL and L % IDX_CHUNK == 0 and (B * L) % FFN_TILE == 0
    assert KV_CHUNK == Q_TILE and IDX_CHUNK == Q_TILE and L % Q_TILE == 0
    assert D_FF % FF_CHUNK == 0 and FF_CHUNK % MXU_TILE == 0 and STEP_UNROLL % 2 == 0
    assert FFN_TILE % FFN_ROWS == 0 and (B * L) % PROJ_TILE == 0 and L % PROJ_TILE == 0
    assert PROJ_TILE % CHUNK == 0 and IDX_CHUNK % FOLD_ROWS == 0 and Q_SUB % FOLD_ROWS == 0

    def vec(a):
        return a.reshape(1, -1)

    h = x.reshape(B * L, D)
    for l in range(depth):
        h = _ffn_ln(h, ffn1_w1[l].astype(bf16), ffn1_w3[l].astype(bf16), ffn1_w2[l].astype(bf16),
                    vec(ln1_g[l]), vec(ln1_b[l]), alpha)
        wuk2 = _pair_blockdiag(jnp.swapaxes(w_uk[l], 1, 2)).astype(bf16)
        wuvt2 = _pair_blockdiag(jnp.swapaxes(w_uv[l], 1, 2)).astype(bf16)
        bs_full = jnp.repeat(gmlp_bs[l].T, MLP_GROUP_DIM, axis=1)
        qabst, c, ct, qit, ki, wit, gm = _proj(
            h, *_split_w_in(w_in[l]), vec(kv_norm_g[l]), wuk2, vec(gmlp_ln_g[l]), vec(gmlp_ln_b[l]),
            gmlp_ws[l], bs_full)
        att = _attn(qabst, qit, wit, ki, c, ct, wuvt2, B, L, topk)
        h = _out_ffn_ln(h, att, gm, w_out[l].astype(bf16), vec(ln2_g[l]), vec(ln2_b[l]),
                        ffn2_w1[l].astype(bf16), ffn2_w3[l].astype(bf16), ffn2_w2[l].astype(bf16),
                        vec(ln3_g[l]), vec(ln3_b[l]), alpha)
    return h.reshape(B, L, D)
```

```python
import functools

import jax
import jax.numpy as jnp
from jax import lax
from jax.experimental import pallas as pl
from jax.experimental.pallas import tpu as pltpu

D_MODEL = 1024
N_ATT_HEADS = 8
ATT_HEAD_DIM = 64
ATT_WIDTH = N_ATT_HEADS * ATT_HEAD_DIM
KV_LATENT = 128
IDX_HEADS = 4
IDX_DIM = 64
TOPK_MAX = 256
N_MLP_GROUPS = 8
MLP_GROUP_DIM = 64
MLP_WIDTH = N_MLP_GROUPS * MLP_GROUP_DIM
CHUNK = 128
D_FF = 2816
LN_EPS = 1e-5

LANES = 128
SUBLANES = 8
MXU_TILE = 256
Q_SUB = LANES
N_SUB = 4
Q_TILE = N_SUB * Q_SUB
N_COLS = N_ATT_HEADS * Q_TILE
IDX_CHUNK = 512
KV_CHUNK = 512
STEP_UNROLL = 2
FOLD_ROWS = 64
FFN_TILE = 512
FFN_ROWS = 256
FF_CHUNK = D_FF
PROJ_TILE = 512
BISECT_STEPS = 22
VMEM_LIMIT = 56 * 1024 * 1024
NEG_BIG = -(2.0 ** 100)
LOG2E = 1.4426950408889634
SUM_ROWS = 16

_PC_Q = 0
_PC_C = ATT_WIDTH
_PC_QI = _PC_C + KV_LATENT
_PC_KW = _PC_QI + IDX_HEADS * IDX_DIM
_PC_Z = _PC_KW + LANES

bf16 = jnp.bfloat16
f32 = jnp.float32


def _const_spec(shape):
    nd = len(shape)
    return pl.BlockSpec(shape, lambda *_: (0,) * nd, pipeline_mode=pl.Buffered(1))


def _layernorm(y, g, b):
    mu = jnp.mean(y, axis=-1, keepdims=True)
    d = y - mu
    var = jnp.mean(d * d, axis=-1, keepdims=True)
    return d * lax.rsqrt(var + LN_EPS) * g + b


def _swiglu(xb, w1_ref, w3_ref, w2_ref):
    acc = None
    for c0 in range(0, D_FF, FF_CHUNK):
        a = jnp.dot(xb, w1_ref[:, c0:c0 + FF_CHUNK], preferred_element_type=f32)
        b = jnp.dot(xb, w3_ref[:, c0:c0 + FF_CHUNK], preferred_element_type=f32)
        h = (jax.nn.silu(a) * b).astype(bf16)
        part = jnp.dot(h, w2_ref[c0:c0 + FF_CHUNK, :], preferred_element_type=f32)
        acc = part if acc is None else acc + part
    return acc


def _ffn_ln_kernel(alpha, x_ref, w1_ref, w3_ref, w2_ref, g_ref, b_ref, o_ref):
    for r0 in range(0, FFN_TILE, FFN_ROWS):
        x = x_ref[r0:r0 + FFN_ROWS, :]
        y = alpha * x + 0.5 * _swiglu(x.astype(bf16), w1_ref, w3_ref, w2_ref)
        o_ref[r0:r0 + FFN_ROWS, :] = _layernorm(y, g_ref[...], b_ref[...])


def _ffn_ln(x, w1, w3, w2, g, b, alpha):
    T = x.shape[0]
    row = pl.BlockSpec((FFN_TILE, D_MODEL), lambda i: (i, 0))
    return pl.pallas_call(
        functools.partial(_ffn_ln_kernel, alpha),
        grid=(T // FFN_TILE,),
        in_specs=[row, _const_spec(w1.shape), _const_spec(w3.shape), _const_spec(w2.shape),
                  _const_spec(g.shape), _const_spec(b.shape)],
        out_specs=row,
        out_shape=jax.ShapeDtypeStruct((T, D_MODEL), f32),
        compiler_params=pltpu.CompilerParams(dimension_semantics=("arbitrary",),
                                             vmem_limit_bytes=VMEM_LIMIT),
        name="ffn_ln",
    )(x, w1, w3, w2, g, b)


def _proj_kernel(h_ref, wh_ref, wz_ref, kvg_ref, wuk_ref, glng_ref, glnb_ref, ws_ref, bs_ref,
                 qabst_ref, c_ref, ct_ref, qit_ref, ki_ref, wit_ref, gm_ref):
    hb = h_ref[...].astype(bf16)
    proj = jnp.dot(hb, wh_ref[...], preferred_element_type=f32)
    proj_z = jnp.dot(hb, wz_ref[...], preferred_element_type=f32)

    for j in range(N_ATT_HEADS // 2):
        qj = proj[:, _PC_Q + j * LANES:_PC_Q + (j + 1) * LANES].astype(bf16)
        qa = jnp.dot(qj, wuk_ref[j], preferred_element_type=f32) * (ATT_HEAD_DIM ** -0.5 * LOG2E)
        for t in range(PROJ_TILE // Q_SUB):
            qabst_ref[t, 2 * j * KV_LATENT:2 * (j + 1) * KV_LATENT, :] = (
                qa[t * Q_SUB:(t + 1) * Q_SUB, :].T.astype(bf16))

    c = proj[:, _PC_C:_PC_C + KV_LATENT]
    c = c * lax.rsqrt(jnp.mean(c * c, axis=-1, keepdims=True) + LN_EPS) * kvg_ref[...]
    c_ref[...] = c.astype(bf16)
    ones_row = lax.broadcasted_iota(jnp.int32, (SUM_ROWS, PROJ_TILE), 0) == 0
    ct_ref[...] = jnp.concatenate([c.T, jnp.where(ones_row, 1.0, 0.0)], axis=0).astype(bf16)

    qi = proj[:, _PC_QI:_PC_QI + IDX_HEADS * IDX_DIM]
    kw = proj[:, _PC_KW:_PC_KW + LANES]
    ki_ref[...] = kw[:, :IDX_DIM].astype(bf16)
    for t in range(PROJ_TILE // Q_SUB):
        qit_ref[t] = qi[t * Q_SUB:(t + 1) * Q_SUB, :].T.astype(bf16)
        wit_ref[t] = kw[t * Q_SUB:(t + 1) * Q_SUB, :].T[IDX_DIM:IDX_DIM + SUBLANES, :]

    z = jax.nn.gelu(proj_z)
    u = z[:, :MLP_WIDTH]
    v = _layernorm(z[:, MLP_WIDTH:], glng_ref[...], glnb_ref[...]).astype(bf16)
    r = lax.broadcasted_iota(jnp.int32, (CHUNK, CHUNK), 0)
    s = lax.broadcasted_iota(jnp.int32, (CHUNK, CHUNK), 1)
    tri = s <= r
    left = lax.broadcasted_iota(jnp.int32, (CHUNK, LANES), 1) < MLP_GROUP_DIM
    for t0 in range(0, PROJ_TILE, CHUNK):
        for j in range(N_MLP_GROUPS // 2):
            vj = v[t0:t0 + CHUNK, j * LANES:(j + 1) * LANES]
            wa = jnp.where(tri, ws_ref[2 * j], 0.0).astype(bf16)
            wb = jnp.where(tri, ws_ref[2 * j + 1], 0.0).astype(bf16)
            sa = jnp.dot(wa, vj, preferred_element_type=f32)
            sb = jnp.dot(wb, vj, preferred_element_type=f32)
            sv = jnp.where(left, sa, sb) + bs_ref[:, j * LANES:(j + 1) * LANES]
            gm_ref[t0:t0 + CHUNK, j * LANES:(j + 1) * LANES] = (
                u[t0:t0 + CHUNK, j * LANES:(j + 1) * LANES] * sv).astype(bf16)


def _proj(h, wh, wz, kvg, wuk2, glng, glnb, ws, bs_full):
    T = h.shape[0]
    tm = PROJ_TILE

    def rows(n, dt):
        return pl.BlockSpec((tm, n), lambda i: (i, 0)), jax.ShapeDtypeStruct((T, n), dt)

    def cols(n, dt):
        return pl.BlockSpec((n, tm), lambda i: (0, i)), jax.ShapeDtypeStruct((n, T), dt)

    def qtiles(n, dt):
        return (pl.BlockSpec((tm // Q_SUB, n, Q_SUB), lambda i: (i, 0, 0)),
                jax.ShapeDtypeStruct((T // Q_SUB, n, Q_SUB), dt))

    outs = [qtiles(N_ATT_HEADS * KV_LATENT, bf16),
            rows(KV_LATENT, bf16),
            cols(KV_LATENT + SUM_ROWS, bf16),
            qtiles(IDX_HEADS * IDX_DIM, bf16),
            rows(IDX_DIM, bf16),
            qtiles(SUBLANES, f32),
            rows(MLP_WIDTH, bf16)]
    return pl.pallas_call(
        _proj_kernel,
        grid=(T // tm,),
        in_specs=[pl.BlockSpec((tm, D_MODEL), lambda i: (i, 0)),
                  _const_spec(wh.shape), _const_spec(wz.shape), _const_spec(kvg.shape),
                  _const_spec(wuk2.shape),
                  _const_spec(glng.shape), _const_spec(glnb.shape), _const_spec(ws.shape),
                  _const_spec(bs_full.shape)],
        out_specs=[o[0] for o in outs],
        out_shape=[o[1] for o in outs],
        compiler_params=pltpu.CompilerParams(dimension_semantics=("arbitrary",),
                                             vmem_limit_bytes=VMEM_LIMIT),
        name="proj",
    )(h, wh, wz, kvg, wuk2, glng, glnb, ws, bs_full)


def _ffn_ln_proj_kernel(alpha, x_ref, w1_ref, w3_ref, w2_ref, g_ref, b_ref, *rest):
    h_ref = rest[8]
    _ffn_ln_kernel(alpha, x_ref, w1_ref, w3_ref, w2_ref, g_ref, b_ref, h_ref)
    _proj_kernel(h_ref, *rest[:8], *rest[9:])


def _ffn_ln_proj(x, w1, w3, w2, g, b, alpha, wh, wz, kvg, wuk2, glng, glnb, ws, bs_full):
    T = x.shape[0]
    tm = FFN_TILE
    row = pl.BlockSpec((tm, D_MODEL), lambda i: (i, 0))

    def rows(n, dt):
        return pl.BlockSpec((tm, n), lambda i: (i, 0)), jax.ShapeDtypeStruct((T, n), dt)

    def cols(n, dt):
        return pl.BlockSpec((n, tm), lambda i: (0, i)), jax.ShapeDtypeStruct((n, T), dt)

    def qtiles(n, dt):
        return (pl.BlockSpec((tm // Q_SUB, n, Q_SUB), lambda i: (i, 0, 0)),
                jax.ShapeDtypeStruct((T // Q_SUB, n, Q_SUB), dt))

    outs = [rows(D_MODEL, f32),
            qtiles(N_ATT_HEADS * KV_LATENT, bf16),
            rows(KV_LATENT, bf16),
            cols(KV_LATENT + SUM_ROWS, bf16),
            qtiles(IDX_HEADS * IDX_DIM, bf16),
            rows(IDX_DIM, bf16),
            qtiles(SUBLANES, f32),
            rows(MLP_WIDTH, bf16)]
    consts = [w1, w3, w2, g, b, wh, wz, kvg, wuk2, glng, glnb, ws, bs_full]
    return pl.pallas_call(
        functools.partial(_ffn_ln_proj_kernel, alpha),
        grid=(T // tm,),
        in_specs=[row] + [_const_spec(a.shape) for a in consts],
        out_specs=[o[0] for o in outs],
        out_shape=[o[1] for o in outs],
        compiler_params=pltpu.CompilerParams(dimension_semantics=("arbitrary",),
                                             vmem_limit_bytes=VMEM_LIMIT),
        name="ffn_ln_proj",
    )(x, *consts)


def _attn_kernel(topk, qabst_ref, qit_ref, wit_ref, ki_ref, c_ref, ct_ref, wuvt_ref, o_ref,
                 st_ref, tau_ref, sca_ref, scb_ref, acc_ref, m_ref, mx_ref):
    i = pl.program_id(1)
    n_keys = (i + 1) * Q_TILE
    n_idx = (n_keys + IDX_CHUNK - 1) // IDX_CHUNK
    n_kv = (n_keys + KV_CHUNK - 1) // KV_CHUNK
    qpos = i * Q_TILE + lax.broadcasted_iota(jnp.int32, (1, Q_TILE), 1)
    kf = float(topk)
    fold_shape = (FOLD_ROWS, Q_TILE)

    def fold_init(v):
        return jnp.full(fold_shape, v, f32)

    def fold_blocks(acc, s, fn):
        for t in range(0, s.shape[0], FOLD_ROWS):
            acc = fn(acc, s[t:t + FOLD_ROWS])
        return acc

    qcat = jnp.concatenate([qit_ref[u, h * IDX_DIM:(h + 1) * IDX_DIM, :]
                            for u in range(N_SUB) for h in range(IDX_HEADS)], axis=1)
    wit = [wit_ref[u] * (IDX_HEADS ** -0.5 * IDX_DIM ** -0.5) for u in range(N_SUB)]

    def scores(j):
        k0 = pl.multiple_of(j * IDX_CHUNK, IDX_CHUNK)
        lg = jnp.dot(ki_ref[pl.ds(k0, IDX_CHUNK), :], qcat, preferred_element_type=f32)
        parts = []
        for u in range(N_SUB):
            s = None
            for h in range(IDX_HEADS):
                col = (u * IDX_HEADS + h) * Q_SUB
                term = wit[u][h:h + 1, :] * jnp.maximum(lg[:, col:col + Q_SUB], 0.0)
                s = term if s is None else s + term
            parts.append(s)
        return k0, jnp.concatenate(parts, axis=1)

    def causal_chunk(j, carry):
        mn, mx = carry
        k0, s = scores(j)
        st_ref[pl.ds(k0, IDX_CHUNK), :] = s
        return fold_blocks(mn, s, jnp.minimum), fold_blocks(mx, s, jnp.maximum)

    def causal_pair(u, carry):
        return causal_chunk(2 * u + 1, causal_chunk(2 * u, carry))

    n_full = n_idx - 1
    mn, mx = lax.fori_loop(0, n_full // 2, causal_pair, (fold_init(jnp.inf), fold_init(-jnp.inf)))
    mn, mx = lax.cond(n_full % 2 == 1, lambda c: causal_chunk(n_full - 1, c), lambda c: c, (mn, mx))
    k0, s = scores(n_idx - 1)
    causal = k0 + lax.broadcasted_iota(jnp.int32, (IDX_CHUNK, 1), 0) <= qpos
    st_ref[pl.ds(k0, IDX_CHUNK), :] = jnp.where(causal, s, -jnp.inf)
    mn = fold_blocks(mn, jnp.where(causal, s, jnp.inf), jnp.minimum)
    mx = fold_blocks(mx, jnp.where(causal, s, -jnp.inf), jnp.maximum)

    def fold_keys(inits, fn, rows):
        full = tuple(jnp.broadcast_to(r, fold_shape) for r in rows)

        def body(j, accs):
            k0 = pl.multiple_of(j * IDX_CHUNK, IDX_CHUNK)
            for t in range(0, IDX_CHUNK, FOLD_ROWS):
                accs = fn(accs, st_ref[pl.ds(k0 + t, FOLD_ROWS), :], *full)
            return accs

        accs = lax.fori_loop(0, n_idx - 1, body, tuple(fold_init(v) for v in inits))
        k0 = pl.multiple_of((n_idx - 1) * IDX_CHUNK, IDX_CHUNK)
        parts = []
        for u in range(N_SUB):
            cols = slice(u * Q_SUB, (u + 1) * Q_SUB)
            part = tuple(a[:, cols] for a in accs)
            sub = tuple(f[:, cols] for f in full)
            for t in range(0, (u + 1) * Q_SUB, FOLD_ROWS):
                part = fn(part, st_ref[pl.ds(k0 + t, FOLD_ROWS), cols], *sub)
            parts.append(part)
        return tuple(jnp.concatenate([p[a] for p in parts], axis=1) for a in range(len(inits)))

    def count_ge(v):
        acc, = fold_keys((0.0,), lambda a, s, vb: (a[0] + jnp.where(s >= vb, 1.0, 0.0),), (v,))
        return jnp.sum(acc, axis=0, keepdims=True)

    @pl.when(i * Q_TILE + Q_TILE <= topk)
    def _():
        tau_ref[...] = jnp.full((1, Q_TILE), -jnp.inf, f32)

    @pl.when(i * Q_TILE + Q_TILE > topk)
    def _():
        lo = jnp.min(mn, axis=0, keepdims=True)
        top = jnp.max(mx, axis=0, keepdims=True)
        hi = top + jnp.maximum(jnp.abs(top) * 1e-3, 1e-30)
        n_lo = (qpos + 1).astype(f32)

        def bisect(_, state):
            lo, hi, n_lo = state
            mid = 0.5 * lo + 0.5 * hi
            n_mid = count_ge(mid)
            ge = n_mid >= kf
            return jnp.where(ge, mid, lo), jnp.where(ge, hi, mid), jnp.where(ge, n_mid, n_lo)

        def count_gt(tau):
            n_gt, = fold_keys((0.0,), lambda g, s, taub: (
                g[0] + jnp.where(s > taub, 1.0, 0.0),), (tau,))
            n_gt = jnp.sum(n_gt, axis=0, keepdims=True)
            return n_gt, jnp.max(jnp.where(n_gt >= kf, 1.0, 0.0))

        lo, hi, n_lo = lax.fori_loop(0, BISECT_STEPS, bisect, (lo, hi, n_lo))
        a, = fold_keys((jnp.inf,), lambda a, s, lob: (
            jnp.minimum(a[0], jnp.where(s >= lob, s, jnp.inf)),), (lo,))
        tau = jnp.min(a, axis=0, keepdims=True)

        def advance(state):
            tau, n_ge, n_gt, _ = state
            nxt, = fold_keys((jnp.inf,), lambda a, s, taub: (
                jnp.minimum(a[0], jnp.where(s > taub, s, jnp.inf)),), (tau,))
            open_row = n_gt >= kf
            tau = jnp.where(open_row, jnp.min(nxt, axis=0, keepdims=True), tau)
            n_ge = jnp.where(open_row, n_gt, n_ge)
            return (tau, n_ge) + count_gt(tau)

        tau, n_ge, n_gt, _ = lax.while_loop(lambda st: st[3] > 0.0, advance,
                                            (tau, n_lo) + count_gt(tau))
        n_eq = n_ge - n_gt
        tau_ref[...] = jnp.where(qpos < topk, -jnp.inf, tau)

        need = kf - n_gt
        surplus = n_eq > need

        @pl.when(jnp.max(jnp.where(surplus, 1.0, 0.0)) > 0.0)
        def _():
            ra = lax.broadcasted_iota(jnp.int32, (IDX_CHUNK, IDX_CHUNK), 0)
            rb = lax.broadcasted_iota(jnp.int32, (IDX_CHUNK, IDX_CHUNK), 1)
            earlier = jnp.where(rb < ra, 1.0, 0.0).astype(bf16)

            def drop(j, seen):
                k0 = pl.multiple_of(j * IDX_CHUNK, IDX_CHUNK)
                s = st_ref[pl.ds(k0, IDX_CHUNK), :]
                eq = jnp.where(s == tau, 1.0, 0.0)
                rank = jnp.dot(earlier, eq.astype(bf16), preferred_element_type=f32) + seen
                st_ref[pl.ds(k0, IDX_CHUNK), :] = jnp.where(
                    s == tau, jnp.where(rank >= need, -jnp.inf, s), s)
                return seen + jnp.sum(eq, axis=0, keepdims=True)

            lax.fori_loop(0, n_idx, drop, jnp.zeros((1, Q_TILE), f32))

    tau = tau_ref[...]
    eye = (lax.broadcasted_iota(jnp.int32, (Q_SUB, Q_SUB), 0)
           == lax.broadcasted_iota(jnp.int32, (Q_SUB, Q_SUB), 1))
    neg_eye = jnp.concatenate([jnp.where(eye, NEG_BIG, 0.0).astype(bf16)] * N_ATT_HEADS, axis=1)
    q_aug = [jnp.concatenate([qabst_ref[u, h * KV_LATENT:(h + 1) * KV_LATENT, :]
                              for h in range(N_ATT_HEADS)], axis=1) for u in range(N_SUB)]
    q_aug = [jnp.concatenate([q, neg_eye], axis=0) for q in q_aug]
    sub_cols = N_ATT_HEADS * Q_SUB

    def stage1(dst, j):
        k0 = pl.multiple_of(j * KV_CHUNK, KV_CHUNK)
        s = st_ref[pl.ds(k0, KV_CHUNK), :]
        kpos = k0 + lax.broadcasted_iota(jnp.int32, (KV_CHUNK, 1), 0)
        dropped = jnp.where(kpos <= qpos, jnp.where(s >= tau, 0.0, 1.0), 1.0).astype(bf16)
        cc = c_ref[pl.ds(k0, KV_CHUNK), :]
        for u in range(N_SUB):
            lhs = jnp.concatenate([cc, dropped[:, u * Q_SUB:(u + 1) * Q_SUB]], axis=1)
            sc = jnp.dot(lhs, q_aug[u], preferred_element_type=f32)
            dst[:, u * sub_cols:(u + 1) * sub_cols] = sc
            mx_ref[:, u * sub_cols:(u + 1) * sub_cols] = jnp.max(sc, axis=0, keepdims=True)

    def stage2(src, j, mx):
        k0 = pl.multiple_of(j * KV_CHUNK, KV_CHUNK)
        m_old = m_ref[...]
        m_new = jnp.maximum(m_old, mx)
        scale = jnp.exp2(m_old - m_new)
        p = jnp.exp2(src[...] - m_new)
        m_ref[...] = m_new
        pv = jnp.dot(ct_ref[:, pl.ds(k0, KV_CHUNK)], p.astype(bf16), preferred_element_type=f32)
        acc_ref[...] = scale * acc_ref[...] + pv

    def stage2_own(src, j, mx):
        k0 = pl.multiple_of(j * KV_CHUNK, KV_CHUNK)
        m_old = m_ref[...]
        m_new = jnp.maximum(m_old, mx)
        scale = jnp.exp2(m_old - m_new)
        m_ref[...] = m_new
        for u in range(N_SUB):
            rows = (u + 1) * Q_SUB
            cols = slice(u * sub_cols, (u + 1) * sub_cols)
            p = jnp.exp2(src[:rows, cols] - m_new[:, cols])
            pv = jnp.dot(ct_ref[:, pl.ds(k0, rows)], p.astype(bf16), preferred_element_type=f32)
            acc_ref[:, cols] = scale[:, cols] * acc_ref[:, cols] + pv

    def step(src, dst, t):
        mx = mx_ref[...]
        stage1(dst, t + 1)
        stage2(src, t, mx)

    m_ref[...] = jnp.full(m_ref.shape, NEG_BIG, f32)
    acc_ref[...] = jnp.zeros(acc_ref.shape, f32)
    stage1(sca_ref, 0)
    n_steps = n_kv - 1

    bufs = (sca_ref, scb_ref)

    def steps(t0, n):
        for r in range(n):
            step(bufs[r % 2], bufs[(r + 1) % 2], t0 + r)

    def unrolled(u, carry):
        steps(STEP_UNROLL * u, STEP_UNROLL)
        return carry

    lax.fori_loop(0, n_steps // STEP_UNROLL, unrolled, 0)
    for rem in range(STEP_UNROLL):
        @pl.when(n_steps % STEP_UNROLL == rem)
        def _(rem=rem):
            steps(n_steps - rem, rem)
            stage2_own(bufs[rem % 2], n_steps, mx_ref[...])

    o_lat = acc_ref[:KV_LATENT, :] / acc_ref[KV_LATENT:KV_LATENT + 1, :]
    for u in range(N_SUB):
        for g in range(N_ATT_HEADS // 2):
            col = u * sub_cols + 2 * g * Q_SUB
            pair = jnp.concatenate([o_lat[:, col:col + Q_SUB], o_lat[:, col + Q_SUB:col + 2 * Q_SUB]],
                                   axis=0).astype(bf16)
            att_t = jnp.dot(wuvt_ref[g], pair, preferred_element_type=f32)
            o_ref[u * Q_SUB:(u + 1) * Q_SUB, g * LANES:(g + 1) * LANES] = att_t.T.astype(bf16)


def _attn(qabst, qit, wit, ki, c, ct, wuvt2, batch, seq, topk):
    T = batch * seq
    nq = seq // Q_TILE

    def qcol(n):
        return pl.BlockSpec((N_SUB, n, Q_SUB), lambda b, i: (b * nq + i, 0, 0))

    return pl.pallas_call(
        functools.partial(_attn_kernel, topk),
        grid=(batch, nq),
        in_specs=[qcol(N_ATT_HEADS * KV_LATENT), qcol(IDX_HEADS * IDX_DIM), qcol(SUBLANES),
                  pl.BlockSpec((seq, IDX_DIM), lambda b, i: (b, 0)),
                  pl.BlockSpec((seq, KV_LATENT), lambda b, i: (b, 0)),
                  pl.BlockSpec((KV_LATENT + SUM_ROWS, seq), lambda b, i: (0, b)),
                  pl.BlockSpec(wuvt2.shape, lambda b, i: (0, 0, 0), pipeline_mode=pl.Buffered(1))],
        out_specs=pl.BlockSpec((Q_TILE, ATT_WIDTH), lambda b, i: (b * nq + i, 0)),
        out_shape=jax.ShapeDtypeStruct((T, ATT_WIDTH), bf16),
        scratch_shapes=[pltpu.VMEM((seq, Q_TILE), f32),
                        pltpu.VMEM((1, Q_TILE), f32),
                        pltpu.VMEM((KV_CHUNK, N_COLS), f32),
                        pltpu.VMEM((KV_CHUNK, N_COLS), f32),
                        pltpu.VMEM((KV_LATENT + SUM_ROWS, N_COLS), f32),
                        pltpu.VMEM((1, N_COLS), f32),
                        pltpu.VMEM((1, N_COLS), f32)],
        compiler_params=pltpu.CompilerParams(dimension_semantics=("arbitrary", "arbitrary"),
                                             vmem_limit_bytes=VMEM_LIMIT),
        name="attn",
    )(qabst, qit, wit, ki, c, ct, wuvt2)


def _out_ffn_ln_kernel(alpha, h_ref, att_ref, gm_ref, wo_ref, g2_ref, b2_ref,
                       w1_ref, w3_ref, w2_ref, g3_ref, b3_ref, o_ref):
    for r0 in range(0, FFN_TILE, FFN_ROWS):
        rows = slice(r0, r0 + FFN_ROWS)
        mix = jnp.concatenate([att_ref[rows, :], gm_ref[rows, :]], axis=1)
        m = jnp.dot(mix, wo_ref[...], preferred_element_type=f32)
        x = _layernorm(alpha * h_ref[rows, :] + m, g2_ref[...], b2_ref[...])
        y = alpha * x + 0.5 * _swiglu(x.astype(bf16), w1_ref, w3_ref, w2_ref)
        o_ref[rows, :] = _layernorm(y, g3_ref[...], b3_ref[...])


def _out_ffn_ln(h, att, gm, wo, g2, b2, w1, w3, w2, g3, b3, alpha):
    T = h.shape[0]
    row = pl.BlockSpec((FFN_TILE, D_MODEL), lambda i: (i, 0))
    half = pl.BlockSpec((FFN_TILE, ATT_WIDTH), lambda i: (i, 0))
    consts = [wo, g2, b2, w1, w3, w2, g3, b3]
    return pl.pallas_call(
        functools.partial(_out_ffn_ln_kernel, alpha),
        grid=(T // FFN_TILE,),
        in_specs=[row, half, half] + [_const_spec(a.shape) for a in consts],
        out_specs=row,
        out_shape=jax.ShapeDtypeStruct((T, D_MODEL), f32),
        compiler_params=pltpu.CompilerParams(dimension_semantics=("arbitrary",),
                                             vmem_limit_bytes=VMEM_LIMIT),
        name="out_ffn_ln",
    )(h, att, gm, *consts)


def _split_w_in(w_in):
    n_head = ATT_WIDTH + KV_LATENT + IDX_HEADS * IDX_DIM + IDX_DIM + IDX_HEADS
    head = jnp.pad(w_in[:, :n_head].astype(bf16), ((0, 0), (0, _PC_Z - n_head)))
    return head, w_in[:, n_head:].astype(bf16)


def _pair_blockdiag(w):
    H, r, c = w.shape
    z = jnp.zeros((H // 2, r, c), w.dtype)
    top = jnp.concatenate([w[0::2], z], axis=2)
    bot = jnp.concatenate([z, w[1::2]], axis=2)
    return jnp.concatenate([top, bot], axis=1)


def kernel(x, ffn1_w1, ffn1_w3, ffn1_w2, ln1_g, ln1_b, w_in, kv_norm_g, w_uk, w_uv, gmlp_ln_g, gmlp_ln_b, gmlp_ws, gmlp_bs, w_out, ln2_g, ln2_b, ffn2_w1, ffn2_w3, ffn2_w2, ln3_g, ln3_b):
    B, L, D = x.shape
    depth = ffn1_w1.shape[0]
    alpha = float((2 * depth) ** 0.25)
    topk = min(TOPK_MAX, L // 4)
    assert D == D_MODEL and L % IDX_CHUNK == 0 and (B * L) % FFN_TILE == 0
    assert KV_CHUNK == Q_TILE and IDX_CHUNK == Q_TILE and L % Q_TILE == 0
    assert D_FF % FF_CHUNK == 0 and FF_CHUNK % MXU_TILE == 0 and STEP_UNROLL % 2 == 0
    assert FFN_TILE % FFN_ROWS == 0 and (B * L) % PROJ_TILE == 0 and L % PROJ_TILE == 0
    assert PROJ_TILE % CHUNK == 0 and IDX_CHUNK % FOLD_ROWS == 0 and Q_SUB % FOLD_ROWS == 0

    def vec(a):
        return a.reshape(1, -1)

    h = x.reshape(B * L, D)
    for l in range(depth):
        wuk2 = _pair_blockdiag(jnp.swapaxes(w_uk[l], 1, 2)).astype(bf16)
        wuvt2 = _pair_blockdiag(jnp.swapaxes(w_uv[l], 1, 2)).astype(bf16)
        bs_full = jnp.repeat(gmlp_bs[l].T, MLP_GROUP_DIM, axis=1)
        h, qabst, c, ct, qit, ki, wit, gm = _ffn_ln_proj(
            h, ffn1_w1[l].astype(bf16), ffn1_w3[l].astype(bf16), ffn1_w2[l].astype(bf16),
            vec(ln1_g[l]), vec(ln1_b[l]), alpha,
            *_split_w_in(w_in[l]), vec(kv_norm_g[l]), wuk2, vec(gmlp_ln_g[l]), vec(gmlp_ln_b[l]),
            gmlp_ws[l], bs_full)
        att = _attn(qabst, qit, wit, ki, c, ct, wuvt2, B, L, topk)
        h = _out_ffn_ln(h, att, gm, w_out[l].astype(bf16), vec(ln2_g[l]), vec(ln2_b[l]),
                        ffn2_w1[l].astype(bf16), ffn2_w3[l].astype(bf16), ffn2_w2[l].astype(bf16),
                        vec(ln3_g[l]), vec(ln3_b[l]), alpha)
    return h.reshape(B, L, D)
```
